```python
import jax, jax.numpy as jnp
from jax import lax
import numpy as np

D_MODEL = 2048
BATCH = 1
SEQ = 8192
DEPTH = 2
DEC_BATCH = 32
DEC_SEQ = 4
PAST_LEN = 8192
PAGE_SIZE = 128

HD = 128
D_A = D_MODEL // 2
CONV_A = 3
H_B = D_MODEL // 2 // HD
KV_B = 2
H_IDX = 16
D_IDX = 64
TOPK_MAX = 256
H_C = D_MODEL // 2 // HD
KV_C = 2
CMP_BLK = 64
N_SEL = 16
WINDOW = 512
D_D = D_MODEL // 2
CONV_D = 31
N_MEM = 256
MEM_H = 4
MEM_HD = 128
D_FF = 43 * D_MODEL // 16
CONV_F = 3
QBLK = 128
N_EVEN = (DEPTH + 1) // 2
N_ODD = DEPTH // 2
EPS = 1e-6
NEG = -1e30
BIG = 1e4
SPLIT_E = (D_A, D_A, D_A, H_B * HD, KV_B * HD, KV_B * HD, H_IDX * D_IDX, D_IDX, H_IDX)
SPLIT_O = (H_C * HD, KV_C * HD, KV_C * HD, KV_C * HD, KV_C * HD, KV_C * HD, KV_C * HD, 3 * H_C, D_D, D_D)
IN_E = sum(SPLIT_E)
IN_O = sum(SPLIT_O)
F32 = jnp.float32

kernel_name = 'hybrid_conv_dsa_nsa_conformer_decode_step'


def split_cols(z, sizes):
    return jnp.split(z, np.cumsum(sizes)[:-1].tolist(), axis=-1)


def rmsnorm(x, g):
    xf = x.astype(F32)
    y = xf * lax.rsqrt(jnp.mean(xf * xf, axis=-1, keepdims=True) + EPS)
    return (y * g.astype(F32)).astype(x.dtype)


def layernorm(x, g, b):
    xf = x.astype(F32)
    mu = jnp.mean(xf, axis=-1, keepdims=True)
    xc = xf - mu
    y = xc * lax.rsqrt(jnp.mean(xc * xc, axis=-1, keepdims=True) + EPS)
    return (y * g.astype(F32) + b.astype(F32)).astype(x.dtype)


def masked_softmax(s, mask):
    s = jnp.where(mask, s.astype(F32), NEG)
    e = jnp.where(mask, jnp.exp(s - jnp.max(s, axis=-1, keepdims=True)), 0.0)
    d = jnp.sum(e, axis=-1, keepdims=True)
    return e / jnp.where(d > 0, d, 1.0)


def causal_dwconv(u, w, prev):
    ext = jnp.concatenate([prev.astype(u.dtype), u], axis=1)
    y = lax.conv_general_dilated(ext, w[:, None, :].astype(u.dtype), (1,), 'VALID',
                                 dimension_numbers=('NWC', 'WIO', 'NWC'),
                                 feature_group_count=u.shape[-1])
    return y, ext[:, ext.shape[1] - (w.shape[0] - 1):]


def gather_pages(cache, page_table):
    g = cache[page_table]
    return g.reshape((page_table.shape[0], page_table.shape[1] * cache.shape[1]) + cache.shape[2:])


def map_query_blocks(fn, t_len):
    out = lax.map(fn, jnp.arange(t_len // QBLK, dtype=jnp.int32) * QBLK)
    return jnp.swapaxes(out, 0, 1).reshape(out.shape[1], t_len, out.shape[-1])


def dsa_attend(q, qpos, qi, wi, k, v, ik, topk):
    b, tq, h, hd = q.shape
    g = k.shape[2]
    kpos = jnp.arange(k.shape[1])
    rel = jax.nn.relu(jnp.einsum('bthd,bsd->bths', qi, ik).astype(F32) * (D_IDX ** -0.5))
    score = jnp.einsum('bth,bths->bts', wi.astype(F32) * (H_IDX ** -0.5), rel)
    score = jnp.where((kpos[None, :] <= qpos[:, None])[None], score, NEG)
    _, sel = lax.top_k(score, topk)
    valid = sel <= qpos[None, :, None]
    gather = jax.vmap(lambda a, i: a[i])
    kg, vg = gather(k, sel), gather(v, sel)
    qg = q.reshape(b, tq, g, h // g, hd)
    s = jnp.einsum('btgrd,btkgd->bgrtk', qg, kg) * (hd ** -0.5)
    p = masked_softmax(s, valid[:, None, None])
    o = jnp.einsum('bgrtk,btkgd->btgrd', p.astype(vg.dtype), vg)
    return o.reshape(b, tq, h * hd)


def nsa_compress(kr, pe, w1, w2):
    b, n, g, hd = kr.shape
    nb = n // CMP_BLK
    blk = kr[:, :nb * CMP_BLK].reshape(b, nb, CMP_BLK, g, hd) + pe[:, None, :]
    flat = blk.transpose(0, 1, 3, 2, 4).reshape(b, nb, g, CMP_BLK * hd)
    return jax.nn.relu(flat @ w1) @ w2


def sel_blocks(kr):
    b, n, g, hd = kr.shape
    nbs = -(-n // CMP_BLK)
    kr = jnp.pad(kr, ((0, 0), (0, nbs * CMP_BLK - n), (0, 0), (0, 0)))
    return kr.reshape(b, nbs, CMP_BLK, g, hd).transpose(0, 3, 1, 2, 4)


def nsa_attend(q, gates, qpos, kc, vc, ksb, vsb, kw, vw, kwpos):
    b, tq, h, hd = q.shape
    g = kc.shape[2]
    nbc, nbs = kc.shape[1], ksb.shape[2]
    qg = q.reshape(b, tq, g, h // g, hd)
    scale = hd ** -0.5
    cmp_ok = ((jnp.arange(nbc) + 1) * CMP_BLK - 1)[None, :] <= qpos[:, None]
    p_c = masked_softmax(jnp.einsum('btgrd,bngd->bgrtn', qg, kc) * scale, cmp_ok)
    o_c = jnp.einsum('bgrtn,bngd->btgrd', p_c.astype(vc.dtype), vc)
    imp = jnp.pad(jnp.sum(p_c, axis=2), ((0, 0), (0, 0), (0, 0), (0, nbs - nbc)))
    cur = (qpos // CMP_BLK)[:, None]
    jb = jnp.arange(nbs)[None, :]
    blk_score = jnp.where(jb > cur, NEG, jnp.where((jb == cur) | (jb == 0), BIG, imp))
    _, sel = lax.top_k(blk_score, min(N_SEL, nbs))
    gather2 = jax.vmap(jax.vmap(lambda a, i: a[i]))
    ks = gather2(ksb, sel).reshape(b, g, tq, -1, hd)
    vs = gather2(vsb, sel).reshape(b, g, tq, -1, hd)
    spos = (sel[..., None] * CMP_BLK + jnp.arange(CMP_BLK)).reshape(b, g, tq, -1)
    s_ok = spos <= qpos[None, None, :, None]
    p_s = masked_softmax(jnp.einsum('btgrd,bgtkd->bgrtk', qg, ks) * scale, s_ok[:, :, None])
    o_s = jnp.einsum('bgrtk,bgtkd->btgrd', p_s.astype(vs.dtype), vs)
    diff = qpos[:, None] - kwpos[None, :]
    w_ok = (diff >= 0) & (diff <= WINDOW) & (kwpos[None, :] >= 0)
    p_w = masked_softmax(jnp.einsum('btgrd,bsgd->bgrts', qg, kw) * scale, w_ok)
    o_w = jnp.einsum('bgrts,bsgd->btgrd', p_w.astype(vw.dtype), vw)
    gt = gates.reshape(b, tq, g, h // g, 3)
    o = gt[..., 0:1] * o_c + gt[..., 1:2] * o_s + gt[..., 2:3] * o_w
    return o.reshape(b, tq, h * hd)


def even_project(h, w_in):
    b, t, _ = h.shape
    xa, bg, cg, q, k, v, qi, ik, wi = split_cols(h @ w_in, SPLIT_E)
    return (xa, bg, cg, q.reshape(b, t, H_B, HD), k.reshape(b, t, KV_B, HD),
            v.reshape(b, t, KV_B, HD), qi.reshape(b, t, H_IDX, D_IDX), ik, wi)


def even_prompt(h, w_in, conv_w, w_out):
    b, t, _ = h.shape
    xa, bg, cg, q, k, v, qi, ik, wi = even_project(h, w_in)
    c, conv_state = causal_dwconv(cg * xa, conv_w, jnp.zeros((b, CONV_A - 1, D_A), h.dtype))
    o_a = bg * c
    topk = min(TOPK_MAX, t // 4)

    def blk(q0):
        sl = lambda a: lax.dynamic_slice_in_dim(a, q0, QBLK, axis=1)
        return dsa_attend(sl(q), q0 + jnp.arange(QBLK), sl(qi), sl(wi), k, v, ik, topk)

    o_b = map_query_blocks(blk, t)
    out = jnp.concatenate([o_a, o_b], axis=-1) @ w_out
    return out, conv_state, k, v, ik


def even_sample(h, w_in, conv_w, w_out, conv_prev, ck, cv, cik, page_table):
    b, t, _ = h.shape
    past = page_table.shape[1] * PAGE_SIZE
    xa, bg, cg, q, k, v, qi, ik, wi = even_project(h, w_in)
    c, conv_state = causal_dwconv(cg * xa, conv_w, conv_prev)
    o_a = bg * c
    kf = jnp.concatenate([gather_pages(ck, page_table).astype(k.dtype), k], axis=1)
    vf = jnp.concatenate([gather_pages(cv, page_table).astype(v.dtype), v], axis=1)
    ikf = jnp.concatenate([gather_pages(cik, page_table).astype(ik.dtype), ik], axis=1)
    topk = min(TOPK_MAX, (past + t) // 4)
    o_b = dsa_attend(q, past + jnp.arange(t), qi, wi, kf, vf, ikf, topk)
    out = jnp.concatenate([o_a, o_b], axis=-1) @ w_out
    return out, conv_state, k, v, ik


def odd_project(h, w_in):
    b, t, _ = h.shape
    q, kc, vc, ks, vs, kw, vw, gl, dp, dg = split_cols(h @ w_in, SPLIT_O)
    kv = lambda a: a.reshape(b, t, KV_C, HD)
    gates = jax.nn.sigmoid(gl.astype(F32)).astype(h.dtype).reshape(b, t, H_C, 3)
    return q.reshape(b, t, H_C, HD), kv(kc), kv(vc), kv(ks), kv(vs), kv(kw), kv(vw), gates, dp, dg


def conformer_conv(dp, dg, conv_w, conv_b, ln_g, ln_b, prev):
    u = dp * jax.nn.sigmoid(dg)
    c, st = causal_dwconv(u, conv_w, prev)
    return jax.nn.silu(layernorm(c + conv_b, ln_g, ln_b)), st


def odd_prompt(h, w_in, phi_k, phi_v, conv_w, conv_b, ln_g, ln_b, w_out):
    b, t, _ = h.shape
    q, kc, vc, ks, vs, kw, vw, gates, dp, dg = odd_project(h, w_in)
    ckc, cvc = nsa_compress(kc, *phi_k), nsa_compress(vc, *phi_v)
    ksb, vsb = sel_blocks(ks), sel_blocks(vs)
    pad = lambda a: jnp.pad(a, ((0, 0), (WINDOW, 0), (0, 0), (0, 0)))
    kwp, vwp = pad(kw), pad(vw)

    def blk(q0):
        sl = lambda a, n: lax.dynamic_slice_in_dim(a, q0, n, axis=1)
        return nsa_attend(sl(q, QBLK), sl(gates, QBLK), q0 + jnp.arange(QBLK), ckc, cvc, ksb, vsb,
                          sl(kwp, WINDOW + QBLK), sl(vwp, WINDOW + QBLK),
                          q0 - WINDOW + jnp.arange(WINDOW + QBLK))

    o_c = map_query_blocks(blk, t)
    o_d, conv_state = conformer_conv(dp, dg, conv_w, conv_b, ln_g, ln_b,
                                     jnp.zeros((b, CONV_D - 1, D_D), h.dtype))
    out = jnp.concatenate([o_c, o_d], axis=-1) @ w_out
    wb = min(WINDOW, t)
    return out, kc, vc, ks, vs, kw[:, t - wb:], vw[:, t - wb:], conv_state


def odd_sample(h, w_in, phi_k, phi_v, conv_w, conv_b, ln_g, ln_b, w_out,
               c_cmp_k, c_cmp_v, c_sel_k, c_sel_v, win_k, win_v, conv_prev, page_table):
    b, t, _ = h.shape
    past = page_table.shape[1] * PAGE_SIZE
    q, kc, vc, ks, vs, kw, vw, gates, dp, dg = odd_project(h, w_in)
    full = lambda c, new: jnp.concatenate([gather_pages(c, page_table).astype(new.dtype), new], axis=1)
    ckc, cvc = nsa_compress(full(c_cmp_k, kc), *phi_k), nsa_compress(full(c_cmp_v, vc), *phi_v)
    ksb, vsb = sel_blocks(full(c_sel_k, ks)), sel_blocks(full(c_sel_v, vs))
    wb = win_k.shape[1]
    kwf = jnp.concatenate([win_k.astype(kw.dtype), kw], axis=1)
    vwf = jnp.concatenate([win_v.astype(vw.dtype), vw], axis=1)
    kwpos = past - wb + jnp.arange(wb + t)
    o_c = nsa_attend(q, gates, past + jnp.arange(t), ckc, cvc, ksb, vsb, kwf, vwf, kwpos)
    o_d, conv_state = conformer_conv(dp, dg, conv_w, conv_b, ln_g, ln_b, conv_prev)
    out = jnp.concatenate([o_c, o_d], axis=-1) @ w_out
    return out, kc, vc, ks, vs, kwf[:, kwf.shape[1] - wb:], vwf[:, vwf.shape[1] - wb:], conv_state


def mem_kv(mem, g, w_k, w_v):
    m = rmsnorm(mem, g)
    b = mem.shape[0]
    return (m @ w_k).reshape(b, -1, MEM_H, MEM_HD), (m @ w_v).reshape(b, -1, MEM_H, MEM_HD)


def cross_attend(h, w_q, w_o, mk, mv):
    b, t, _ = h.shape
    q = (h @ w_q).reshape(b, t, MEM_H, MEM_HD)
    s = jnp.einsum('bthd,bmhd->bhtm', q, mk.astype(q.dtype)).astype(F32) * (MEM_HD ** -0.5)
    p = jax.nn.softmax(s, axis=-1)
    o = jnp.einsum('bhtm,bmhd->bthd', p.astype(q.dtype), mv.astype(q.dtype))
    return o.reshape(b, t, MEM_H * MEM_HD) @ w_o


def conv_ffn(h, w_up, conv_w, w_down, prev):
    a, v = jnp.split(h @ w_up, 2, axis=-1)
    c, st = causal_dwconv(a, conv_w, prev)
    return (jax.nn.silu(c) * v) @ w_down, st


def setup_inputs(seed: int = 0) -> dict:
    key = jax.random.key(seed)
    keys = iter(jax.random.split(key, 64))

    def nrm(shape, scale=1.0):
        return jax.random.normal(next(keys), shape, F32) * scale

    def gain(shape):
        return 1.0 + 0.05 * nrm(shape)

    n_pages = PAST_LEN // PAGE_SIZE
    n_pool = (DEC_BATCH * n_pages * 5) // 4
    wbuf = min(WINDOW, PAST_LEN)
    out = {}
    out['x_prompt'] = nrm((BATCH, SEQ, D_MODEL))
    out['x_sample'] = nrm((DEC_BATCH, DEC_SEQ, D_MODEL))
    out['state_conv_a'] = nrm((N_EVEN, DEC_BATCH, CONV_A - 1, D_A))
    out['cache_dsa_k'] = nrm((N_EVEN, n_pool, PAGE_SIZE, KV_B, HD))
    out['cache_dsa_v'] = nrm((N_EVEN, n_pool, PAGE_SIZE, KV_B, HD))
    out['cache_dsa_ik'] = nrm((N_EVEN, n_pool, PAGE_SIZE, D_IDX))
    out['cache_nsa_cmp_k'] = nrm((N_ODD, n_pool, PAGE_SIZE, KV_C, HD))
    out['cache_nsa_cmp_v'] = nrm((N_ODD, n_pool, PAGE_SIZE, KV_C, HD))
    out['cache_nsa_sel_k'] = nrm((N_ODD, n_pool, PAGE_SIZE, KV_C, HD))
    out['cache_nsa_sel_v'] = nrm((N_ODD, n_pool, PAGE_SIZE, KV_C, HD))
    out['cache_nsa_win_k'] = nrm((N_ODD, DEC_BATCH, wbuf, KV_C, HD))
    out['cache_nsa_win_v'] = nrm((N_ODD, DEC_BATCH, wbuf, KV_C, HD))
    out['state_conv_d'] = nrm((N_ODD, DEC_BATCH, CONV_D - 1, D_D))
    out['state_ffn_conv'] = nrm((DEPTH, DEC_BATCH, CONV_F - 1, D_FF))
    out['cache_mem_k'] = nrm((DEPTH, DEC_BATCH, N_MEM, MEM_H, MEM_HD))
    out['cache_mem_v'] = nrm((DEPTH, DEC_BATCH, N_MEM, MEM_H, MEM_HD))
    perm = jax.random.permutation(next(keys), n_pool)[:DEC_BATCH * n_pages]
    out['page_table'] = perm.reshape(DEC_BATCH, n_pages).astype(jnp.int32)
    out['mem_prompt'] = nrm((BATCH, N_MEM, D_MODEL))
    out['w_in_e'] = nrm((N_EVEN, D_MODEL, IN_E), D_MODEL ** -0.5)
    out['conv_a_w'] = nrm((N_EVEN, CONV_A, D_A), CONV_A ** -0.5)
    out['w_out_e'] = nrm((N_EVEN, D_A + H_B * HD, D_MODEL), (D_A + H_B * HD) ** -0.5)
    out['w_in_o'] = nrm((N_ODD, D_MODEL, IN_O), D_MODEL ** -0.5)
    out['nsa_pe_k'] = nrm((N_ODD, CMP_BLK, HD), 0.1)
    out['nsa_w1_k'] = nrm((N_ODD, CMP_BLK * HD, HD), (CMP_BLK * HD) ** -0.5)
    out['nsa_w2_k'] = nrm((N_ODD, HD, HD), HD ** -0.5)
    out['nsa_pe_v'] = nrm((N_ODD, CMP_BLK, HD), 0.1)
    out['nsa_w1_v'] = nrm((N_ODD, CMP_BLK * HD, HD), (CMP_BLK * HD) ** -0.5)
    out['nsa_w2_v'] = nrm((N_ODD, HD, HD), HD ** -0.5)
    out['conv_d_w'] = nrm((N_ODD, CONV_D, D_D), CONV_D ** -0.5)
    out['conv_d_b'] = nrm((N_ODD, D_D), 0.02)
    out['ln_d_g'] = gain((N_ODD, D_D))
    out['ln_d_b'] = nrm((N_ODD, D_D), 0.02)
    out['w_out_o'] = nrm((N_ODD, H_C * HD + D_D, D_MODEL), (H_C * HD + D_D) ** -0.5)
    out['norm_g'] = gain((DEPTH, 6, D_MODEL))
    out['mem_norm_g'] = gain((DEPTH, D_MODEL))
    out['w_mq'] = nrm((DEPTH, D_MODEL, MEM_H * MEM_HD), D_MODEL ** -0.5)
    out['w_mk'] = nrm((DEPTH, D_MODEL, MEM_H * MEM_HD), D_MODEL ** -0.5)
    out['w_mv'] = nrm((DEPTH, D_MODEL, MEM_H * MEM_HD), D_MODEL ** -0.5)
    out['w_mo'] = nrm((DEPTH, MEM_H * MEM_HD, D_MODEL), (MEM_H * MEM_HD) ** -0.5)
    out['w_up'] = nrm((DEPTH, D_MODEL, 2 * D_FF), D_MODEL ** -0.5)
    out['ffn_conv_w'] = nrm((DEPTH, CONV_F, D_FF), CONV_F ** -0.5)
    out['w_down'] = nrm((DEPTH, D_FF, D_MODEL), D_FF ** -0.5)
    return out


def reference(x_prompt, x_sample, state_conv_a, cache_dsa_k, cache_dsa_v, cache_dsa_ik,
              cache_nsa_cmp_k, cache_nsa_cmp_v, cache_nsa_sel_k, cache_nsa_sel_v,
              cache_nsa_win_k, cache_nsa_win_v, state_conv_d, state_ffn_conv,
              cache_mem_k, cache_mem_v, page_table, mem_prompt,
              w_in_e, conv_a_w, w_out_e, w_in_o, nsa_pe_k, nsa_w1_k, nsa_w2_k,
              nsa_pe_v, nsa_w1_v, nsa_w2_v, conv_d_w, conv_d_b, ln_d_g, ln_d_b, w_out_o,
              norm_g, mem_norm_g, w_mq, w_mk, w_mv, w_mo, w_up, ffn_conv_w, w_down):
    names = ('p_conv_a', 'p_dsa_k', 'p_dsa_v', 'p_dsa_ik', 'p_cmp_k', 'p_cmp_v', 'p_sel_k', 'p_sel_v',
             'p_win_k', 'p_win_v', 'p_conv_d', 'p_ffn', 'p_mem_k', 'p_mem_v',
             's_conv_a', 's_dsa_k', 's_dsa_v', 's_dsa_ik', 's_cmp_k', 's_cmp_v', 's_sel_k', 's_sel_v',
             's_win_k', 's_win_v', 's_conv_d', 's_ffn')
    st = {n: [] for n in names}

    def push(keys_, arrays):
        for n, a in zip(keys_, arrays):
            st[n].append(a)

    yp, ys = x_prompt, x_sample
    bp = x_prompt.shape[0]
    for l in range(DEPTH):
        g = norm_g[l]
        hp, hs = rmsnorm(yp, g[0]), rmsnorm(ys, g[0])
        i = l // 2
        if l % 2 == 0:
            mp, *sp = even_prompt(hp, w_in_e[i], conv_a_w[i], w_out_e[i])
            push(('p_conv_a', 'p_dsa_k', 'p_dsa_v', 'p_dsa_ik'), sp)
            ms, *ss = even_sample(hs, w_in_e[i], conv_a_w[i], w_out_e[i], state_conv_a[i],
                                  cache_dsa_k[i], cache_dsa_v[i], cache_dsa_ik[i], page_table)
            push(('s_conv_a', 's_dsa_k', 's_dsa_v', 's_dsa_ik'), ss)
        else:
            phi_k = (nsa_pe_k[i], nsa_w1_k[i], nsa_w2_k[i])
            phi_v = (nsa_pe_v[i], nsa_w1_v[i], nsa_w2_v[i])
            mp, *sp = odd_prompt(hp, w_in_o[i], phi_k, phi_v, conv_d_w[i], conv_d_b[i],
                                 ln_d_g[i], ln_d_b[i], w_out_o[i])
            push(('p_cmp_k', 'p_cmp_v', 'p_sel_k', 'p_sel_v', 'p_win_k', 'p_win_v', 'p_conv_d'), sp)
            ms, *ss = odd_sample(hs, w_in_o[i], phi_k, phi_v, conv_d_w[i], conv_d_b[i],
                                 ln_d_g[i], ln_d_b[i], w_out_o[i],
                                 cache_nsa_cmp_k[i], cache_nsa_cmp_v[i], cache_nsa_sel_k[i],
                                 cache_nsa_sel_v[i], cache_nsa_win_k[i], cache_nsa_win_v[i],
                                 state_conv_d[i], page_table)
            push(('s_cmp_k', 's_cmp_v', 's_sel_k', 's_sel_v', 's_win_k', 's_win_v', 's_conv_d'), ss)
        yp = yp + rmsnorm(mp, g[1])
        ys = ys + rmsnorm(ms, g[1])
        mk, mv = mem_kv(mem_prompt, mem_norm_g[l], w_mk[l], w_mv[l])
        push(('p_mem_k', 'p_mem_v'), (mk, mv))
        yp = yp + rmsnorm(cross_attend(rmsnorm(yp, g[2]), w_mq[l], w_mo[l], mk, mv), g[3])
        ys = ys + rmsnorm(cross_attend(rmsnorm(ys, g[2]), w_mq[l], w_mo[l],
                                       cache_mem_k[l], cache_mem_v[l]), g[3])
        fp, fsp = conv_ffn(rmsnorm(yp, g[4]), w_up[l], ffn_conv_w[l], w_down[l],
                           jnp.zeros((bp, CONV_F - 1, D_FF), yp.dtype))
        fs, fss = conv_ffn(rmsnorm(ys, g[4]), w_up[l], ffn_conv_w[l], w_down[l], state_ffn_conv[l])
        push(('p_ffn', 's_ffn'), (fsp, fss))
        yp = yp + rmsnorm(fp, g[5])
        ys = ys + rmsnorm(fs, g[5])
    S = {n: jnp.stack(a) for n, a in st.items()}
    return (yp, ys,
            S['p_conv_a'], S['p_dsa_k'], S['p_dsa_v'], S['p_dsa_ik'],
            S['p_cmp_k'], S['p_cmp_v'], S['p_sel_k'], S['p_sel_v'], S['p_win_k'], S['p_win_v'],
            S['p_conv_d'], S['p_ffn'], S['p_mem_k'], S['p_mem_v'],
            S['s_conv_a'], S['s_dsa_k'], S['s_dsa_v'], S['s_dsa_ik'],
            S['s_cmp_k'], S['s_cmp_v'], S['s_sel_k'], S['s_sel_v'], S['s_win_k'], S['s_win_v'],
            S['s_conv_d'], S['s_ffn'])
```

```python
import functools

import jax
import jax.numpy as jnp
from jax import lax
from jax.experimental import pallas as pl
from jax.experimental.pallas import tpu as pltpu

F32 = jnp.float32
BF16 = jnp.bfloat16
I32 = jnp.int32

EPS = 1e-6
NEG = -1e30
BIG = 1e4
HD = 128
PAGE = 128
CMP_BLK = 64
N_SEL = 16
WINDOW = 512
TOPK_MAX = 256
H_IDX = 16
D_IDX = 64
INT_MIN = -(2 ** 31)

VMEM_LIMIT_BYTES = 56 * 1024 * 1024


def _params(*sem):
    return pltpu.CompilerParams(dimension_semantics=sem, vmem_limit_bytes=VMEM_LIMIT_BYTES)


def _rms(x, g):
    return x * lax.rsqrt(jnp.mean(x * x, axis=-1, keepdims=True) + EPS) * g


def _dot(a, b):
    return jnp.dot(a.astype(BF16), b.astype(BF16), preferred_element_type=F32)


def _dot_nt(a, b):
    return lax.dot_general(a.astype(BF16), b.astype(BF16), (((1,), (1,)), ((), ())),
                           preferred_element_type=F32)


def _norm_mm_kernel(x_ref, g_ref, w_ref, o_ref, xn_ref):
    @pl.when(pl.program_id(1) == 0)
    def _():
        xn_ref[...] = _rms(x_ref[...], g_ref[...]).astype(BF16)

    o_ref[...] = jnp.dot(xn_ref[...], w_ref[...].astype(BF16), preferred_element_type=F32)


def norm_mm(x, g, w, *, tm, tn):
    m, k = x.shape
    n = w.shape[1]
    return pl.pallas_call(
        _norm_mm_kernel,
        grid=(m // tm, pl.cdiv(n, tn)),
        in_specs=[pl.BlockSpec((tm, k), lambda i, j: (i, 0)),
                  pl.BlockSpec((1, k), lambda i, j: (0, 0)),
                  pl.BlockSpec((k, tn), lambda i, j: (0, j))],
        out_specs=pl.BlockSpec((tm, tn), lambda i, j: (i, j)),
        out_shape=jax.ShapeDtypeStruct((m, n), F32),
        scratch_shapes=[pltpu.VMEM((tm, k), BF16)],
        compiler_params=_params("arbitrary", "arbitrary"),
        name="norm_mm",
    )(x, g.reshape(1, k), w)


def _mm_norm_res_kernel(*refs, n_in, widths, tn, nj):
    a_refs = refs[:n_in]
    w_ref, g_ref, y_ref, o_ref, a_bf = refs[n_in:]
    j = pl.program_id(1)

    @pl.when(j == 0)
    def _():
        off = 0
        for a_ref, wd in zip(a_refs, widths):
            a_bf[:, off:off + wd] = a_ref[...].astype(BF16)
            off += wd

    col = pl.multiple_of(j * tn, tn)
    o_ref[:, pl.ds(col, tn)] = jnp.dot(a_bf[...], w_ref[...].astype(BF16),
                                       preferred_element_type=F32)

    @pl.when(j == nj - 1)
    def _():
        o_ref[...] = y_ref[...] + _rms(o_ref[...], g_ref[...])


def mm_norm_res(a_list, w, g, y, *, tm, tn):
    m, n = y.shape
    widths = tuple(a.shape[1] for a in a_list)
    k = sum(widths)
    assert w.shape == (k, n) and n % tn == 0
    nj = n // tn
    kern = functools.partial(_mm_norm_res_kernel, n_in=len(a_list), widths=widths, tn=tn, nj=nj)
    return pl.pallas_call(
        kern,
        grid=(m // tm, nj),
        in_specs=[pl.BlockSpec((tm, wd), lambda i, j: (i, 0)) for wd in widths]
        + [pl.BlockSpec((k, tn), lambda i, j: (0, j)),
           pl.BlockSpec((1, n), lambda i, j: (0, 0)),
           pl.BlockSpec((tm, n), lambda i, j: (i, 0))],
        out_specs=pl.BlockSpec((tm, n), lambda i, j: (i, 0)),
        out_shape=jax.ShapeDtypeStruct((m, n), F32),
        scratch_shapes=[pltpu.VMEM((tm, k), BF16)],
        compiler_params=_params("arbitrary", "arbitrary"),
        name="mm_norm_res",
    )(*a_list, w, g.reshape(1, n), y)


def _order_key(x):
    b = pltpu.bitcast(x, I32)
    return b ^ ((b >> 31) & 0x7FFFFFFF)


def _kth_largest_key(count_ge, k, shape):
    t0 = jnp.where(count_ge(jnp.zeros(shape, I32)) >= k, 0, INT_MIN).astype(I32)

    def bit_body(n, t):
        cand = t | jnp.left_shift(jnp.int32(1), 30 - n)
        return jnp.where(count_ge(cand) >= k, cand, t)

    return lax.fori_loop(0, 31, bit_body, t0)


def _dsa_prompt_kernel(q_ref, qi_ref, wi_ref, k_ref, v_ref, ik_ref, o_ref, key_sc,
                       *, tq, tk, topk, n_heads, n_groups):
    i = pl.program_id(0)
    q0 = i * tq
    nch = (q0 + tq + tk - 1) // tk
    rep = n_heads // n_groups
    qpos = q0 + lax.broadcasted_iota(I32, (tq, 1), 0)

    wiw = wi_ref[...] * (H_IDX ** -0.5 * D_IDX ** -0.5)
    qi_bf = qi_ref[...].astype(BF16)

    def score_body(c, carry):
        col = pl.multiple_of(c * tk, tk)
        ikc = ik_ref[pl.ds(col, tk), :]
        acc = jnp.zeros((tq, tk), F32)
        for h in range(H_IDX):
            s = _dot_nt(qi_bf[:, h * D_IDX:(h + 1) * D_IDX], ikc)
            acc = acc + wiw[:, h:h + 1] * jnp.maximum(s, 0.0)
        kpos = col + lax.broadcasted_iota(I32, (1, tk), 1)
        key_sc[:, pl.ds(col, tk)] = jnp.where(kpos <= qpos, _order_key(acc), INT_MIN)
        return carry

    lax.fori_loop(0, nch, score_body, 0)

    def count_ge(t):
        def body(c, cnt):
            col = pl.multiple_of(c * tk, tk)
            return cnt + jnp.where(key_sc[:, pl.ds(col, tk)] >= t, 1, 0)

        cnt = lax.fori_loop(0, nch, body, jnp.zeros((tq, tk), I32))
        return jnp.sum(cnt, axis=1, keepdims=True)

    thr = jnp.maximum(_kth_largest_key(count_ge, topk, (tq, 1)), INT_MIN + 1)

    scale = HD ** -0.5
    for g in range(n_groups):
        qg = jnp.concatenate(
            [q_ref[:, (g * rep + r) * HD:(g * rep + r + 1) * HD] for r in range(rep)], axis=0
        ).astype(BF16)

        def att_body(c, carry, g=g, qg=qg):
            m, l, acc = carry
            col = pl.multiple_of(c * tk, tk)
            kc = k_ref[pl.ds(col, tk), g * HD:(g + 1) * HD]
            vc = v_ref[pl.ds(col, tk), g * HD:(g + 1) * HD]
            s = _dot_nt(qg, kc) * scale
            sel = jnp.where(key_sc[:, pl.ds(col, tk)] >= thr, 1, 0)
            sel = jnp.concatenate([sel] * rep, axis=0) > 0
            s = jnp.where(sel, s, NEG)
            m_new = jnp.maximum(m, jnp.max(s, axis=1, keepdims=True))
            p = jnp.where(sel, jnp.exp(s - m_new), 0.0)
            alpha = jnp.exp(m - m_new)
            l = alpha * l + jnp.sum(p, axis=1, keepdims=True)
            acc = alpha * acc + _dot(p, vc)
            return m_new, l, acc

        init = (jnp.full((rep * tq, 1), NEG, F32), jnp.zeros((rep * tq, 1), F32),
                jnp.zeros((rep * tq, HD), F32))
        _, l, acc = lax.fori_loop(0, nch, att_body, init)
        o = acc / jnp.where(l > 0, l, 1.0)
        for r in range(rep):
            o_ref[:, (g * rep + r) * HD:(g * rep + r + 1) * HD] = o[r * tq:(r + 1) * tq, :]


def dsa_prompt(q, qi, wi, k_bf, v_bf, ik_bf, *, tq=128, tk=256):
    t, dq = q.shape
    n_heads = dq // HD
    n_groups = k_bf.shape[1] // HD
    topk = min(TOPK_MAX, t // 4)
    kern = functools.partial(_dsa_prompt_kernel, tq=tq, tk=tk, topk=topk,
                             n_heads=n_heads, n_groups=n_groups)
    full = lambda a: pl.BlockSpec(a.shape, lambda i: (0, 0))
    row = lambda a: pl.BlockSpec((tq, a.shape[1]), lambda i: (i, 0))
    return pl.pallas_call(
        kern,
        grid=(t // tq,),
        in_specs=[row(q), row(qi), row(wi), full(k_bf), full(v_bf), full(ik_bf)],
        out_specs=pl.BlockSpec((tq, dq), lambda i: (i, 0)),
        out_shape=jax.ShapeDtypeStruct((t, dq), F32),
        scratch_shapes=[pltpu.VMEM((tq, t), I32)],
        compiler_params=_params("arbitrary"),
        name="dsa_prompt",
    )(q, qi, wi, k_bf, v_bf, ik_bf)


def _nsa_compress_kernel(x_ref, pe_ref, w1_ref, w2_ref, o_ref, *, nblk, n_groups):
    _compress_rows(x_ref, pe_ref, w1_ref, w2_ref, o_ref, nblk=nblk, n_groups=n_groups)


def _compress_rows(x_ref, pe_ref, w1_ref, w2_ref, o_ref, *, nblk, n_groups):
    acc = jnp.zeros((n_groups * nblk, HD), F32)
    for p in range(CMP_BLK):
        pe_p = pe_ref[p:p + 1, :]
        lhs = jnp.concatenate(
            [x_ref[pl.ds(p * n_groups + g, nblk, stride=CMP_BLK * n_groups), :] + pe_p
             for g in range(n_groups)], axis=0)
        acc = acc + _dot(lhs, w1_ref[p * HD:(p + 1) * HD, :])
    out = _dot(jnp.maximum(acc, 0.0), w2_ref[...])
    for g in range(n_groups):
        o_ref[:, g * HD:(g + 1) * HD] = out[g * nblk:(g + 1) * nblk, :]


def nsa_compress(x, pe, w1, w2, n_groups):
    b, tg, _ = x.shape
    nblk = tg // (n_groups * CMP_BLK)
    kern = functools.partial(_nsa_compress_kernel, nblk=nblk, n_groups=n_groups)
    full = lambda a: pl.BlockSpec(a.shape, lambda i: (0,) * a.ndim)
    return pl.pallas_call(
        kern,
        grid=(b,),
        in_specs=[pl.BlockSpec((None, tg, HD), lambda i: (i, 0, 0)), full(pe), full(w1), full(w2)],
        out_specs=pl.BlockSpec((None, nblk, n_groups * HD), lambda i: (i, 0, 0)),
        out_shape=jax.ShapeDtypeStruct((b, nblk, n_groups * HD), F32),
        compiler_params=_params("arbitrary"),
        name="nsa_compress",
    )(x, pe, w1, w2)


def _masked_softmax(s, ok):
    s = jnp.where(ok, s, NEG)
    e = jnp.where(ok, jnp.exp(s - jnp.max(s, axis=-1, keepdims=True)), 0.0)
    d = jnp.sum(e, axis=-1, keepdims=True)
    return e / jnp.where(d > 0, d, 1.0)


def _top_blocks(score, n_sel):
    nb = score.shape[1]
    col = lax.broadcasted_iota(I32, score.shape, 1)
    sel = jnp.zeros(score.shape, F32)
    for _ in range(n_sel):
        m = jnp.max(score, axis=1, keepdims=True)
        idx = jnp.min(jnp.where(score == m, col, nb), axis=1, keepdims=True)
        hit = col == idx
        sel = jnp.where(hit, 1.0, sel)
        score = jnp.where(hit, -3e38, score)
    return sel


def _nsa_prompt_kernel(q_ref, gl_ref, ck_ref, cv_ref, ks_ref, vs_ref, kw_ref, vw_ref, o_ref,
                       *, tq, tk, n_heads, n_groups, nbc):
    i = pl.program_id(0)
    q0 = i * tq
    rep = n_heads // n_groups
    scale = HD ** -0.5
    qpos = q0 + lax.broadcasted_iota(I32, (tq, 1), 0)
    jb = lax.broadcasted_iota(I32, (1, nbc), 1)
    cmp_ok = jnp.concatenate([((jb + 1) * CMP_BLK - 1) <= qpos] * rep, axis=0)
    cur = qpos // CMP_BLK
    gates = jax.nn.sigmoid(gl_ref[...])

    qgs, o_cs, imps = [], [], []
    for g in range(n_groups):
        qg = jnp.concatenate(
            [q_ref[:, (g * rep + r) * HD:(g * rep + r + 1) * HD] for r in range(rep)], axis=0
        ).astype(BF16)
        p_c = _masked_softmax(_dot_nt(qg, ck_ref[g]) * scale, cmp_ok)
        o_cs.append(_dot(p_c, cv_ref[g]))
        imp = p_c[0:tq]
        for r in range(1, rep):
            imp = imp + p_c[r * tq:(r + 1) * tq]
        qgs.append(qg)
        imps.append(imp)

    imp = jnp.concatenate(imps, axis=0)
    cur2 = jnp.concatenate([cur] * n_groups, axis=0)
    blk_score = jnp.where(jb > cur2, NEG, jnp.where((jb == cur2) | (jb == 0), BIG, imp))
    selm = _top_blocks(blk_score, min(N_SEL, nbc)).astype(BF16)

    nch = (q0 + tq + tk - 1) // tk
    bpc = tk // CMP_BLK
    w0 = pl.multiple_of(jnp.maximum(q0 - WINDOW, 0), tq)
    wlen = WINDOW + tq
    for g in range(n_groups):
        qg = qgs[g]
        selg = selm[g * tq:(g + 1) * tq]

        def att_body(c, carry, g=g, qg=qg, selg=selg):
            m, l, acc = carry
            col = pl.multiple_of(c * tk, tk)
            kc = ks_ref[pl.ds(col, tk), g * HD:(g + 1) * HD]
            vc = vs_ref[pl.ds(col, tk), g * HD:(g + 1) * HD]
            s = _dot_nt(qg, kc) * scale
            brow = lax.broadcasted_iota(I32, (nbc, tk), 0)
            bcol = c * bpc + lax.broadcasted_iota(I32, (nbc, tk), 1) // CMP_BLK
            expand = jnp.where(brow == bcol, 1.0, 0.0).astype(BF16)
            picked = jnp.dot(selg, expand, preferred_element_type=F32)
            kpos = col + lax.broadcasted_iota(I32, (1, tk), 1)
            ok = jnp.where((picked > 0.5) & (kpos <= qpos), 1, 0)
            ok = jnp.concatenate([ok] * rep, axis=0) > 0
            s = jnp.where(ok, s, NEG)
            m_new = jnp.maximum(m, jnp.max(s, axis=1, keepdims=True))
            p = jnp.where(ok, jnp.exp(s - m_new), 0.0)
            alpha = jnp.exp(m - m_new)
            l = alpha * l + jnp.sum(p, axis=1, keepdims=True)
            acc = alpha * acc + _dot(p, vc)
            return m_new, l, acc

        init = (jnp.full((rep * tq, 1), NEG, F32), jnp.zeros((rep * tq, 1), F32),
                jnp.zeros((rep * tq, HD), F32))
        _, l, acc = lax.fori_loop(0, nch, att_body, init)
        o_s = acc / jnp.where(l > 0, l, 1.0)

        kw = kw_ref[pl.ds(w0, wlen), g * HD:(g + 1) * HD]
        vw = vw_ref[pl.ds(w0, wlen), g * HD:(g + 1) * HD]
        diff = qpos - (w0 + lax.broadcasted_iota(I32, (1, wlen), 1))
        w_ok = jnp.where((diff >= 0) & (diff <= WINDOW), 1, 0)
        w_ok = jnp.concatenate([w_ok] * rep, axis=0) > 0
        p_w = _masked_softmax(_dot_nt(qg, kw) * scale, w_ok)
        o_w = _dot(p_w, vw)

        o_c = o_cs[g]
        for r in range(rep):
            h = g * rep + r
            rows = slice(r * tq, (r + 1) * tq)
            o_ref[:, h * HD:(h + 1) * HD] = (gates[:, 3 * h:3 * h + 1] * o_c[rows]
                                             + gates[:, 3 * h + 1:3 * h + 2] * o_s[rows]
                                             + gates[:, 3 * h + 2:3 * h + 3] * o_w[rows])


def nsa_prompt(q, gl, ck_bf, cv_bf, ks_bf, vs_bf, kw_bf, vw_bf, *, tq=128, tk=512):
    t, dq = q.shape
    n_groups, nbc, _ = ck_bf.shape
    assert t >= WINDOW + tq and t % tk == 0 and nbc == t // CMP_BLK
    kern = functools.partial(_nsa_prompt_kernel, tq=tq, tk=tk, n_heads=dq // HD,
                             n_groups=n_groups, nbc=nbc)
    full = lambda a: pl.BlockSpec(a.shape, lambda i: (0,) * a.ndim)
    row = lambda a: pl.BlockSpec((tq, a.shape[1]), lambda i: (i, 0))
    return pl.pallas_call(
        kern,
        grid=(t // tq,),
        in_specs=[row(q), row(gl), full(ck_bf), full(cv_bf), full(ks_bf), full(vs_bf),
                  full(kw_bf), full(vw_bf)],
        out_specs=pl.BlockSpec((tq, dq), lambda i: (i, 0)),
        out_shape=jax.ShapeDtypeStruct((t, dq), F32),
        compiler_params=_params("arbitrary"),
        name="nsa_prompt",
    )(q, gl, ck_bf, cv_bf, ks_bf, vs_bf, kw_bf, vw_bf)


def _halo_rows(kw, shift):
    return max(8, (kw - 1) * shift)


def _dwconv(ext_ref, w_ref, u, prev_ref, first, *, tm, kw, shift):
    hp = _halo_rows(kw, shift)

    @pl.when(first)
    def _():
        ext_ref[0:hp, :] = prev_ref[...]

    ext_ref[hp:hp + tm, :] = u
    y = None
    for i in range(kw):
        term = w_ref[i:i + 1, :] * ext_ref[pl.ds(hp - (kw - 1 - i) * shift, tm), :]
        y = term if y is None else y + term
    tail = ext_ref[tm:tm + hp, :]
    ext_ref[0:hp, :] = tail
    return y, tail


def _gated_conv_kernel(xa_ref, bg_ref, cg_ref, w_ref, prev_ref, o_ref, st_ref, ext_ref,
                       *, tm, kw, shift):
    u = cg_ref[...] * xa_ref[...]
    y, tail = _dwconv(ext_ref, w_ref, u, prev_ref, pl.program_id(0) == 0, tm=tm, kw=kw, shift=shift)
    o_ref[...] = bg_ref[...] * y
    st_ref[...] = tail


def gated_conv(z, c, w, prev, *, tm, shift):
    m = z.shape[0]
    kw = w.shape[0]
    hp = _halo_rows(kw, shift)
    kern = functools.partial(_gated_conv_kernel, tm=tm, kw=kw, shift=shift)
    col = lambda j: pl.BlockSpec((tm, c), lambda i: (i, j))
    return pl.pallas_call(
        kern,
        grid=(m // tm,),
        in_specs=[col(0), col(1), col(2), pl.BlockSpec((kw, c), lambda i: (0, 0)),
                  pl.BlockSpec((hp, c), lambda i: (0, 0))],
        out_specs=[pl.BlockSpec((tm, c), lambda i: (i, 0)), pl.BlockSpec((hp, c), lambda i: (0, 0))],
        out_shape=[jax.ShapeDtypeStruct((m, c), F32), jax.ShapeDtypeStruct((hp, c), F32)],
        scratch_shapes=[pltpu.VMEM((hp + tm, c), F32)],
        compiler_params=_params("arbitrary"),
        name="gated_conv",
    )(z, z, z, w, prev)


def _conformer_kernel(dp_ref, dg_ref, w_ref, b_ref, lg_ref, lb_ref, prev_ref, o_ref, st_ref, ext_ref,
                      *, tm, kw, shift):
    u = dp_ref[...] * jax.nn.sigmoid(dg_ref[...])
    c, tail = _dwconv(ext_ref, w_ref, u, prev_ref, pl.program_id(0) == 0, tm=tm, kw=kw, shift=shift)
    c = c + b_ref[...]
    mu = jnp.mean(c, axis=-1, keepdims=True)
    xc = c - mu
    y = xc * lax.rsqrt(jnp.mean(xc * xc, axis=-1, keepdims=True) + EPS) * lg_ref[...] + lb_ref[...]
    o_ref[...] = y * jax.nn.sigmoid(y)
    st_ref[...] = tail


def conformer_conv(dp, dg, w, b, ln_g, ln_b, prev, *, tm, shift):
    m, c = dp.shape
    kw = w.shape[0]
    hp = _halo_rows(kw, shift)
    kern = functools.partial(_conformer_kernel, tm=tm, kw=kw, shift=shift)
    row = pl.BlockSpec((tm, c), lambda i: (i, 0))
    vec = pl.BlockSpec((1, c), lambda i: (0, 0))
    return pl.pallas_call(
        kern,
        grid=(m // tm,),
        in_specs=[row, row, pl.BlockSpec((kw, c), lambda i: (0, 0)), vec, vec, vec,
                  pl.BlockSpec((hp, c), lambda i: (0, 0))],
        out_specs=[row, pl.BlockSpec((hp, c), lambda i: (0, 0))],
        out_shape=[jax.ShapeDtypeStruct((m, c), F32), jax.ShapeDtypeStruct((hp, c), F32)],
        scratch_shapes=[pltpu.VMEM((hp + tm, c), F32)],
        compiler_params=_params("arbitrary"),
        name="conformer_conv",
    )(dp, dg, w, b.reshape(1, c), ln_g.reshape(1, c), ln_b.reshape(1, c), prev)


FFN_TF = 128


def _ffn_up_kernel(y_ref, g_ref, wa_ref, wv_ref, cw_ref, prev_ref, o_ref, st_ref,
                   xn_ref, wcat_ref, halo_ref, ext_ref, *, tm, kw, shift):
    i = pl.program_id(0)
    j = pl.program_id(1)
    hp = _halo_rows(kw, shift)

    @pl.when(j == 0)
    def _():
        xn_ref[...] = _rms(y_ref[...], g_ref[...]).astype(BF16)

    wcat_ref[:, 0:FFN_TF] = wa_ref[...].astype(BF16)
    wcat_ref[:, FFN_TF:2 * FFN_TF] = wv_ref[...].astype(BF16)
    h = jnp.dot(xn_ref[...], wcat_ref[...], preferred_element_type=F32)
    a = h[:, 0:FFN_TF]
    v = h[:, FFN_TF:2 * FFN_TF]

    @pl.when(i > 0)
    def _():
        ext_ref[0:hp, :] = halo_ref[j]

    c, tail = _dwconv(ext_ref, cw_ref, a, prev_ref, i == 0, tm=tm, kw=kw, shift=shift)
    halo_ref[j] = tail
    st_ref[...] = tail
    o_ref[...] = (c * jax.nn.sigmoid(c) * v).astype(BF16)


def ffn_up(y, g, w_up, conv_w, prev, *, tm, shift):
    m, d = y.shape
    kw, f = conv_w.shape
    nf = f // FFN_TF
    assert f % FFN_TF == 0 and w_up.shape == (d, 2 * f)
    hp = _halo_rows(kw, shift)
    nm = m // tm
    kern = functools.partial(_ffn_up_kernel, tm=tm, kw=kw, shift=shift)
    return pl.pallas_call(
        kern,
        grid=(nm, nf),
        in_specs=[pl.BlockSpec((tm, d), lambda i, j: (i, 0)),
                  pl.BlockSpec((1, d), lambda i, j: (0, 0)),
                  pl.BlockSpec((d, FFN_TF), lambda i, j: (0, j)),
                  pl.BlockSpec((d, FFN_TF), lambda i, j: (0, nf + j)),
                  pl.BlockSpec((kw, FFN_TF), lambda i, j: (0, j)),
                  pl.BlockSpec((hp, FFN_TF), lambda i, j: (0, j))],
        out_specs=[pl.BlockSpec((tm, FFN_TF), lambda i, j: (i, j)),
                   pl.BlockSpec((hp, FFN_TF), lambda i, j: (i, j))],
        out_shape=[jax.ShapeDtypeStruct((m, f), BF16), jax.ShapeDtypeStruct((nm * hp, f), F32)],
        scratch_shapes=[pltpu.VMEM((tm, d), BF16), pltpu.VMEM((d, 2 * FFN_TF), BF16),
                        pltpu.VMEM((nf, hp, FFN_TF), F32), pltpu.VMEM((hp + tm, FFN_TF), F32)],
        compiler_params=_params("arbitrary", "arbitrary"),
        name="ffn_up",
    )(y, g.reshape(1, d), w_up, w_up, conv_w, prev)


def _mem_heads(q, mk_ref, mv_ref, n_heads):
    scale = HD ** -0.5
    outs = []
    for h in range(n_heads):
        s = _dot_nt(q[:, h * HD:(h + 1) * HD], mk_ref[:, h * HD:(h + 1) * HD]) * scale
        e = jnp.exp(s - jnp.max(s, axis=-1, keepdims=True))
        p = e / jnp.sum(e, axis=-1, keepdims=True)
        outs.append(_dot(p, mv_ref[:, h * HD:(h + 1) * HD]))
    return jnp.concatenate(outs, axis=1)


def _xattn_kernel(y_ref, g2_ref, wq_ref, mk_ref, mv_ref, wo_ref, g3_ref, o_ref, wq_bf, wo_bf,
                  *, n_heads):
    @pl.when(pl.program_id(0) == 0)
    def _():
        wq_bf[...] = wq_ref[...].astype(BF16)
        wo_bf[...] = wo_ref[...].astype(BF16)

    y = y_ref[...]
    q = jnp.dot(_rms(y, g2_ref[...]).astype(BF16), wq_bf[...], preferred_element_type=F32)
    o = _mem_heads(q, mk_ref, mv_ref, n_heads)
    f = jnp.dot(o.astype(BF16), wo_bf[...], preferred_element_type=F32)
    o_ref[...] = y + _rms(f, g3_ref[...])


def cross_attn(y, g2, wq, mk, mv, wo, g3, *, tm):
    m, d = y.shape
    dh = wq.shape[1]
    kern = functools.partial(_xattn_kernel, n_heads=dh // HD)
    full = lambda a: pl.BlockSpec(a.shape, lambda i: (0, 0))
    row = pl.BlockSpec((tm, d), lambda i: (i, 0))
    vec = pl.BlockSpec((1, d), lambda i: (0, 0))
    return pl.pallas_call(
        kern,
        grid=(m // tm,),
        in_specs=[row, vec, full(wq), full(mk), full(mv), full(wo), vec],
        out_specs=row,
        out_shape=jax.ShapeDtypeStruct((m, d), F32),
        scratch_shapes=[pltpu.VMEM(wq.shape, BF16), pltpu.VMEM(wo.shape, BF16)],
        compiler_params=_params("arbitrary"),
        name="cross_attn",
    )(y, g2.reshape(1, d), wq, mk, mv, wo, g3.reshape(1, d))


def _mem_attn_kernel(q_ref, mk_ref, mv_ref, o_ref, *, n_heads):
    o_ref[...] = _mem_heads(q_ref[...], mk_ref, mv_ref, n_heads)


def mem_attn_batched(q, mk, mv):
    b, t, dh = q.shape
    kern = functools.partial(_mem_attn_kernel, n_heads=dh // HD)
    blk = lambda a: pl.BlockSpec((None,) + a.shape[1:], lambda i: (i, 0, 0))
    return pl.pallas_call(
        kern,
        grid=(b,),
        in_specs=[blk(q), blk(mk), blk(mv)],
        out_specs=blk(q),
        out_shape=jax.ShapeDtypeStruct(q.shape, F32),
        compiler_params=_params("arbitrary"),
        name="mem_attn_batched",
    )(q, mk, mv)


def _page_copy(cache, pt_ref, b, p, buf, slot, sem):
    rows = cache.shape[1]
    return pltpu.make_async_copy(cache.at[pt_ref[b, p]],
                                 buf.at[slot, pl.ds(p * rows, rows), :], sem.at[slot])


def _for_pages(caches, bufs, sems, pt_ref, b, slot, n_pages, fn):
    def body(p, carry):
        for cache, buf, sem in zip(caches, bufs, sems):
            fn(_page_copy(cache, pt_ref, b, p, buf, slot, sem))
        return carry

    lax.fori_loop(0, n_pages, body, 0)


def _fetch_pages(caches, bufs, sems, pt_ref, n_pages):
    b = pl.program_id(0)
    slot = b % 2
    args = (caches, bufs, sems, pt_ref)

    @pl.when(b == 0)
    def _():
        _for_pages(*args, b, slot, n_pages, lambda cp: cp.start())

    @pl.when(b + 1 < pl.num_programs(0))
    def _():
        _for_pages(*args, b + 1, 1 - slot, n_pages, lambda cp: cp.start())

    _for_pages(*args, b, slot, n_pages, lambda cp: cp.wait())
    return slot


def _paged_call(kern, pt, inputs, n_any, out_shape, out_block, scratch, name):
    nb = pt.shape[0]

    def spec(a):
        nd = a.ndim - 1
        return pl.BlockSpec((None,) + a.shape[1:], lambda i, pt_ref: (i,) + (0,) * nd)

    def shared(a):
        nd = a.ndim
        return pl.BlockSpec(a.shape, lambda i, pt_ref: (0,) * nd)

    in_specs = []
    for a in inputs[:len(inputs) - n_any]:
        in_specs.append(spec(a) if a.shape[0] == nb and a.ndim >= 3 else shared(a))
    in_specs += [pl.BlockSpec(memory_space=pl.ANY)] * n_any
    nd_o = len(out_block)
    return pl.pallas_call(
        kern,
        grid_spec=pltpu.PrefetchScalarGridSpec(
            num_scalar_prefetch=1,
            grid=(nb,),
            in_specs=in_specs,
            out_specs=pl.BlockSpec((None,) + tuple(out_block), lambda i, pt_ref: (i,) + (0,) * nd_o),
            scratch_shapes=scratch),
        out_shape=jax.ShapeDtypeStruct((nb,) + tuple(out_block), out_shape),
        compiler_params=_params("arbitrary"),
        name=name,
    )(pt, *inputs)


def _softmax_step(qg, kc, vc, ok, carry, scale):
    m, l, acc = carry
    s = jnp.where(ok, _dot_nt(qg, kc) * scale, NEG)
    m_new = jnp.maximum(m, jnp.max(s, axis=1, keepdims=True))
    p = jnp.where(ok, jnp.exp(s - m_new), 0.0)
    alpha = jnp.exp(m - m_new)
    return m_new, alpha * l + jnp.sum(p, axis=1, keepdims=True), alpha * acc + _dot(p, vc)


def _softmax_init(rows):
    return (jnp.full((rows, 1), NEG, F32), jnp.zeros((rows, 1), F32), jnp.zeros((rows, HD), F32))


def _dsa_sample_select_kernel(pt_ref, qi_ref, wi_ref, ikn_ref, ik_hbm, m_ref, ikbuf, key_sc, sem,
                              *, n_pages, tk, topk, n_new, nq, rows):
    slot = _fetch_pages([ik_hbm], [ikbuf], [sem], pt_ref, n_pages)
    past = n_pages * PAGE
    qi = qi_ref[...].astype(BF16)
    w = wi_ref[...] * (H_IDX ** -0.5 * D_IDX ** -0.5)

    def scores(ikc):
        r = jnp.maximum(_dot_nt(qi, ikc), 0.0) * w
        return jnp.sum(r.reshape(H_IDX, rows, ikc.shape[0]), axis=0)

    def score_body(c, carry):
        col = pl.multiple_of(c * tk, tk)
        key_sc[:, pl.ds(col, tk)] = _order_key(scores(ikbuf[slot, pl.ds(col, tk), :]))
        return carry

    lax.fori_loop(0, past // tk, score_body, 0)
    t_row = lax.rem(lax.broadcasted_iota(I32, (rows, PAGE), 0), nq)
    col = lax.broadcasted_iota(I32, (rows, PAGE), 1)
    new_ok = (col < n_new) & (col <= t_row)
    key_sc[:, past:past + PAGE] = jnp.where(new_ok, _order_key(scores(ikn_ref[...])), INT_MIN)

    def count_ge(t):
        def body(c, cnt):
            col = pl.multiple_of(c * tk, tk)
            return cnt + jnp.where(key_sc[:, pl.ds(col, tk)] >= t, 1, 0)

        cnt = lax.fori_loop(0, past // tk, body, jnp.zeros((rows, tk), I32))
        cnt_new = jnp.where(key_sc[:, past:past + PAGE] >= t, 1, 0)
        return jnp.sum(cnt, axis=1, keepdims=True) + jnp.sum(cnt_new, axis=1, keepdims=True)

    thr = jnp.maximum(_kth_largest_key(count_ge, topk, (rows, 1)), INT_MIN + 1)
    m_ref[...] = jnp.where(key_sc[...] >= thr, 1.0, 0.0)


def _dsa_sample_attn_kernel(pt_ref, q_ref, m_ref, kn_ref, vn_ref, k_hbm, v_hbm, o_ref,
                            kbuf, vbuf, ksem, vsem, *, n_pages, tk, n_groups):
    slot = _fetch_pages([k_hbm, v_hbm], [kbuf, vbuf], [ksem, vsem], pt_ref, n_pages)
    past = n_pages * PAGE
    scale = HD ** -0.5
    rows = q_ref.shape[1]
    for g in range(n_groups):
        qg = q_ref[g].astype(BF16)

        def body(c, carry, g=g, qg=qg):
            col = pl.multiple_of(c * tk, tk)
            kc = kbuf[slot, pl.ds(col * n_groups + g, tk, stride=n_groups), :]
            vc = vbuf[slot, pl.ds(col * n_groups + g, tk, stride=n_groups), :]
            return _softmax_step(qg, kc, vc, m_ref[:, pl.ds(col, tk)] > 0.5, carry, scale)

        carry = lax.fori_loop(0, past // tk, body, _softmax_init(rows))
        kc = kn_ref[pl.ds(g, PAGE, stride=n_groups), :]
        vc = vn_ref[pl.ds(g, PAGE, stride=n_groups), :]
        _, l, acc = _softmax_step(qg, kc, vc, m_ref[:, past:past + PAGE] > 0.5, carry, scale)
        o_ref[g] = acc / jnp.where(l > 0, l, 1.0)


def _group_rows(x, n_groups):
    b, t, dq = x.shape
    rep = dq // HD // n_groups
    return x.reshape(b, t, n_groups, rep, HD).transpose(0, 2, 3, 1, 4).reshape(b, n_groups, rep * t, HD)


def _ungroup_rows(o, t):
    b, g, rows, _ = o.shape
    rep = rows // t
    return o.reshape(b, g, rep, t, HD).transpose(0, 3, 1, 2, 4).reshape(b, t, g * rep * HD)


def _pad_new_rows(x, n_groups):
    b, t, _ = x.shape
    x = x.reshape(b, t * n_groups, HD)
    return jnp.pad(x, ((0, 0), (0, (PAGE - t) * n_groups), (0, 0)))


def dsa_sample(q, qi, wi, k_new, v_new, ik_new, cache_k, cache_v, cache_ik, pt, *, tk=512):
    nb, nq, dq = q.shape
    n_pages = pt.shape[1]
    n_groups = cache_k.shape[2]
    rep = dq // HD // n_groups
    rows = rep * nq
    past = n_pages * PAGE
    topk = min(TOPK_MAX, (past + nq) // 4)
    width = past + PAGE
    qi_r = jnp.broadcast_to(qi.reshape(nb, nq, H_IDX, D_IDX).transpose(0, 2, 1, 3)[:, :, None],
                            (nb, H_IDX, rep, nq, D_IDX)).reshape(nb, H_IDX * rows, D_IDX)
    wi_r = jnp.broadcast_to(wi.transpose(0, 2, 1)[:, :, None], (nb, H_IDX, rep, nq)).reshape(nb, H_IDX * rows, 1)
    ikn = jnp.pad(ik_new, ((0, 0), (0, PAGE - nq), (0, 0)))
    sel_kern = functools.partial(_dsa_sample_select_kernel, n_pages=n_pages, tk=tk, topk=topk,
                                 n_new=nq, nq=nq, rows=rows)
    mask = _paged_call(
        sel_kern, pt, [qi_r, wi_r, ikn, cache_ik], 1, F32, (rows, width),
        [pltpu.VMEM((2, past, D_IDX), F32), pltpu.VMEM((rows, width), I32),
         pltpu.SemaphoreType.DMA((2,))], "dsa_sample_select")
    ck = cache_k.reshape(cache_k.shape[0], PAGE * n_groups, HD)
    cv = cache_v.reshape(cache_v.shape[0], PAGE * n_groups, HD)
    att_kern = functools.partial(_dsa_sample_attn_kernel, n_pages=n_pages, tk=tk, n_groups=n_groups)
    o = _paged_call(
        att_kern, pt, [_group_rows(q, n_groups), mask, _pad_new_rows(k_new, n_groups),
                       _pad_new_rows(v_new, n_groups), ck, cv], 2, F32, (n_groups, rows, HD),
        [pltpu.VMEM((2, past * n_groups, HD), F32), pltpu.VMEM((2, past * n_groups, HD), F32),
         pltpu.SemaphoreType.DMA((2,)), pltpu.SemaphoreType.DMA((2,))], "dsa_sample_attn")
    return _ungroup_rows(o, nq)


def _nsa_compress_paged_kernel(pt_ref, pe_ref, w1_ref, w2_ref, x_hbm, o_ref, buf, sem,
                               *, n_pages, n_groups):
    slot = _fetch_pages([x_hbm], [buf], [sem], pt_ref, n_pages)
    _compress_rows(buf.at[slot], pe_ref, w1_ref, w2_ref, o_ref,
                   nblk=n_pages * PAGE // CMP_BLK, n_groups=n_groups)


def nsa_compress_paged(cache, pt, pe, w1, w2):
    n_groups = cache.shape[2]
    n_pages = pt.shape[1]
    past = n_pages * PAGE
    kern = functools.partial(_nsa_compress_paged_kernel, n_pages=n_pages, n_groups=n_groups)
    view = cache.reshape(cache.shape[0], PAGE * n_groups, HD)
    return _paged_call(
        kern, pt, [pe, w1.astype(BF16), w2, view], 1, F32, (past // CMP_BLK, n_groups * HD),
        [pltpu.VMEM((2, past * n_groups, HD), F32), pltpu.SemaphoreType.DMA((2,))],
        "nsa_compress_paged")


def _nsa_sample_kernel(pt_ref, q_ref, gl_ref, ck_ref, cv_ref, wk_ref, wv_ref, skn_ref, svn_ref,
                       wkn_ref, wvn_ref, ks_hbm, vs_hbm, o_ref, kbuf, vbuf, ksem, vsem,
                       *, n_pages, tk, n_groups, nq, n_new):
    slot = _fetch_pages([ks_hbm, vs_hbm], [kbuf, vbuf], [ksem, vsem], pt_ref, n_pages)
    past = n_pages * PAGE
    scale = HD ** -0.5
    rows = q_ref.shape[1]
    rep = rows // nq
    nbc = ck_ref.shape[0]
    wlen = wk_ref.shape[0] // n_groups
    bpc = tk // CMP_BLK
    t_row = lax.rem(lax.broadcasted_iota(I32, (rows, 1), 0), nq)
    qpos = past + t_row
    jb = lax.broadcasted_iota(I32, (1, nbc), 1)
    cmp_ok = ((jb + 1) * CMP_BLK - 1) <= qpos
    cur = qpos // CMP_BLK
    newcol = lax.broadcasted_iota(I32, (1, PAGE), 1)
    new_ok = (newcol < n_new) & (newcol <= t_row)
    diff = qpos - (past - wlen + lax.broadcasted_iota(I32, (1, wlen), 1))
    win_ok = (diff >= 0) & (diff <= WINDOW)
    for g in range(n_groups):
        cols = slice(g * HD, (g + 1) * HD)
        qg = q_ref[g].astype(BF16)
        gates = jax.nn.sigmoid(gl_ref[g])
        p_c = _masked_softmax(_dot_nt(qg, ck_ref[:, cols]) * scale, cmp_ok)
        o_c = _dot(p_c, cv_ref[:, cols])
        imp = p_c
        for r in range(1, rep):
            imp = imp + pltpu.roll(p_c, r * nq, 0)
        blk_score = jnp.where(jb > cur, NEG, jnp.where((jb == cur) | (jb == 0), BIG, imp))
        selm = _top_blocks(blk_score, N_SEL - 1).astype(BF16)

        def body(c, carry, g=g, qg=qg, selm=selm):
            col = pl.multiple_of(c * tk, tk)
            kc = kbuf[slot, pl.ds(col * n_groups + g, tk, stride=n_groups), :]
            vc = vbuf[slot, pl.ds(col * n_groups + g, tk, stride=n_groups), :]
            brow = lax.broadcasted_iota(I32, (nbc, tk), 0)
            bcol = c * bpc + lax.broadcasted_iota(I32, (nbc, tk), 1) // CMP_BLK
            expand = jnp.where(brow == bcol, 1.0, 0.0).astype(BF16)
            picked = jnp.dot(selm, expand, preferred_element_type=F32)
            return _softmax_step(qg, kc, vc, picked > 0.5, carry, scale)

        carry = lax.fori_loop(0, past // tk, body, _softmax_init(rows))
        kc = skn_ref[pl.ds(g, PAGE, stride=n_groups), :]
        vc = svn_ref[pl.ds(g, PAGE, stride=n_groups), :]
        _, l, acc = _softmax_step(qg, kc, vc, new_ok, carry, scale)
        o_s = acc / jnp.where(l > 0, l, 1.0)
        kw = wk_ref[pl.ds(g, wlen, stride=n_groups), :]
        vw = wv_ref[pl.ds(g, wlen, stride=n_groups), :]
        kwn = wkn_ref[pl.ds(g, PAGE, stride=n_groups), :]
        vwn = wvn_ref[pl.ds(g, PAGE, stride=n_groups), :]
        s_c = jnp.where(win_ok, _dot_nt(qg, kw) * scale, NEG)
        s_n = jnp.where(new_ok, _dot_nt(qg, kwn) * scale, NEG)
        mx = jnp.maximum(jnp.max(s_c, axis=1, keepdims=True), jnp.max(s_n, axis=1, keepdims=True))
        e_c = jnp.where(win_ok, jnp.exp(s_c - mx), 0.0)
        e_n = jnp.where(new_ok, jnp.exp(s_n - mx), 0.0)
        den = jnp.sum(e_c, axis=1, keepdims=True) + jnp.sum(e_n, axis=1, keepdims=True)
        den = jnp.where(den > 0, den, 1.0)
        o_w = _dot(e_c / den, vw) + _dot(e_n / den, vwn)
        o_ref[g] = gates[:, 0:1] * o_c + gates[:, 1:2] * o_s + gates[:, 2:3] * o_w


def nsa_sample(q, gl, ck, cv, win_k, win_v, ks_new, vs_new, kw_new, vw_new, cache_ks, cache_vs, pt,
               *, tk=512):
    nb, nq, dq = q.shape
    n_pages = pt.shape[1]
    n_groups = cache_ks.shape[2]
    n_heads = dq // HD
    rep = n_heads // n_groups
    rows = rep * nq
    past = n_pages * PAGE
    assert past % CMP_BLK == 0 and nq <= CMP_BLK and ck.shape[1] == past // CMP_BLK
    assert win_k.shape[1] == WINDOW
    glr = gl.reshape(nb, nq, n_groups, rep, 3).transpose(0, 2, 3, 1, 4).reshape(nb, n_groups, rows, 3)
    view = lambda c: c.reshape(c.shape[0], c.shape[1] * n_groups, HD)
    kern = functools.partial(_nsa_sample_kernel, n_pages=n_pages, tk=tk, n_groups=n_groups,
                             nq=nq, n_new=nq)
    pad = lambda x: _pad_new_rows(x, n_groups)
    o = _paged_call(
        kern, pt, [_group_rows(q, n_groups), glr, ck, cv, view(win_k), view(win_v),
                   pad(ks_new), pad(vs_new), pad(kw_new), pad(vw_new), view(cache_ks), view(cache_vs)],
        2, F32, (n_groups, rows, HD),
        [pltpu.VMEM((2, past * n_groups, HD), F32), pltpu.VMEM((2, past * n_groups, HD), F32),
         pltpu.SemaphoreType.DMA((2,)), pltpu.SemaphoreType.DMA((2,))], "nsa_sample")
    return _ungroup_rows(o, nq)


def _tmajor(a):
    b, t, c = a.shape
    return jnp.swapaxes(a, 0, 1).reshape(t * b, c)


def _bmajor(a, b):
    tb, c = a.shape
    return jnp.swapaxes(a.reshape(tb // b, b, c), 0, 1)


def _col_splits(z, sizes):
    out, off = [], 0
    for s in sizes:
        out.append(z[..., off:off + s])
        off += s
    return out


def _row_tile(m, pref):
    return pref if m % pref == 0 else m


def kernel(x_prompt, x_sample, state_conv_a, cache_dsa_k, cache_dsa_v, cache_dsa_ik, cache_nsa_cmp_k, cache_nsa_cmp_v, cache_nsa_sel_k, cache_nsa_sel_v, cache_nsa_win_k, cache_nsa_win_v, state_conv_d, state_ffn_conv, cache_mem_k, cache_mem_v, page_table, mem_prompt, w_in_e, conv_a_w, w_out_e, w_in_o, nsa_pe_k, nsa_w1_k, nsa_w2_k, nsa_pe_v, nsa_w1_v, nsa_w2_v, conv_d_w, conv_d_b, ln_d_g, ln_d_b, w_out_o, norm_g, mem_norm_g, w_mq, w_mk, w_mv, w_mo, w_up, ffn_conv_w, w_down):
    bp, seq, d = x_prompt.shape
    nb, nq, _ = x_sample.shape
    assert bp == 1
    depth = norm_g.shape[0]
    ts = nb * nq
    d_a = conv_a_w.shape[-1]
    d_d = conv_d_w.shape[-1]
    d_ff = ffn_conv_w.shape[-1]
    kv_b = cache_dsa_k.shape[-2] * HD
    kv_c = cache_nsa_cmp_k.shape[-2] * HD
    n_kv_c = cache_nsa_cmp_k.shape[-2]
    dq_b = w_out_e.shape[1] - d_a
    dq_c = w_out_o.shape[1] - d_d
    split_e = (d_a, d_a, d_a, dq_b, kv_b, kv_b, H_IDX * D_IDX, D_IDX, H_IDX)
    split_o = (dq_c,) + (kv_c,) * 6 + (3 * dq_c // HD, d_d, d_d)
    tm_big = _row_tile(seq, 1024)
    tm_mid = _row_tile(seq, 512)
    wb = min(WINDOW, seq)

    yp = x_prompt.reshape(seq, d)
    ys = _tmajor(x_sample)
    mem = mem_prompt.reshape(mem_prompt.shape[1], d)
    names = ('p_conv_a', 'p_dsa_k', 'p_dsa_v', 'p_dsa_ik', 'p_cmp_k', 'p_cmp_v', 'p_sel_k', 'p_sel_v',
             'p_win_k', 'p_win_v', 'p_conv_d', 'p_ffn', 'p_mem_k', 'p_mem_v',
             's_conv_a', 's_dsa_k', 's_dsa_v', 's_dsa_ik', 's_cmp_k', 's_cmp_v', 's_sel_k', 's_sel_v',
             's_win_k', 's_win_v', 's_conv_d', 's_ffn')
    st = {n: [] for n in names}

    for l in range(depth):
        g = norm_g[l]
        i = l // 2
        if l % 2 == 0:
            w_in = w_in_e[i]
            zp = norm_mm(yp, g[0], w_in, tm=tm_big, tn=512)
            zs = norm_mm(ys, g[0], w_in, tm=ts, tn=512)
            kw_a = conv_a_w.shape[1]
            o_a, hist = gated_conv(zp, d_a, conv_a_w[i], jnp.zeros((_halo_rows(kw_a, 1), d_a), F32),
                                   tm=tm_mid, shift=1)
            st['p_conv_a'].append(hist[hist.shape[0] - (kw_a - 1):][None])
            _, _, _, q, k, v, qi, ik, wi = _col_splits(zp, split_e)
            o_b = dsa_prompt(q, qi, wi, k.astype(BF16), v.astype(BF16), ik.astype(BF16))
            yp = mm_norm_res([o_a, o_b], w_out_e[i], g[1], yp, tm=tm_mid, tn=512)
            st['p_dsa_k'].append(k.reshape(1, seq, -1, HD))
            st['p_dsa_v'].append(v.reshape(1, seq, -1, HD))
            st['p_dsa_ik'].append(ik[None])
            o_a, hist = gated_conv(zs, d_a, conv_a_w[i], _tmajor(state_conv_a[i]), tm=ts, shift=nb)
            st['s_conv_a'].append(_bmajor(hist, nb))
            _, _, _, q, k, v, qi, ik, wi = _col_splits(_bmajor(zs, nb), split_e)
            o_b = dsa_sample(q, qi, wi, k, v, ik, cache_dsa_k[i], cache_dsa_v[i], cache_dsa_ik[i],
                             page_table)
            ys = mm_norm_res([o_a, _tmajor(o_b)], w_out_e[i], g[1], ys, tm=ts, tn=512)
            st['s_dsa_k'].append(k.reshape(nb, nq, -1, HD))
            st['s_dsa_v'].append(v.reshape(nb, nq, -1, HD))
            st['s_dsa_ik'].append(ik)
        else:
            w_in = w_in_o[i]
            zp = norm_mm(yp, g[0], w_in, tm=tm_big, tn=512)
            zs = norm_mm(ys, g[0], w_in, tm=ts, tn=512)
            phi_k = (nsa_pe_k[i], nsa_w1_k[i], nsa_w2_k[i])
            phi_v = (nsa_pe_v[i], nsa_w1_v[i], nsa_w2_v[i])
            kw_d = conv_d_w.shape[1]
            q, kc, vc, ks, vs, kw, vw, gl, dp, dg = _col_splits(zp, split_o)
            rows2 = lambda a: a.reshape(1, seq * n_kv_c, HD)
            by_group = lambda c: c.reshape(-1, n_kv_c, HD).transpose(1, 0, 2).astype(BF16)
            ck = by_group(nsa_compress(rows2(kc), *phi_k, n_kv_c))
            cv = by_group(nsa_compress(rows2(vc), *phi_v, n_kv_c))
            o_c = nsa_prompt(q, gl, ck, cv, ks.astype(BF16), vs.astype(BF16),
                             kw.astype(BF16), vw.astype(BF16))
            o_d, hist = conformer_conv(dp, dg, conv_d_w[i], conv_d_b[i], ln_d_g[i], ln_d_b[i],
                                       jnp.zeros((_halo_rows(kw_d, 1), d_d), F32), tm=tm_mid, shift=1)
            st['p_conv_d'].append(hist[hist.shape[0] - (kw_d - 1):][None])
            yp = mm_norm_res([o_c, o_d], w_out_o[i], g[1], yp, tm=tm_mid, tn=512)
            kv4 = lambda a: a.reshape(1, -1, n_kv_c, HD)
            for n, a in zip(('p_cmp_k', 'p_cmp_v', 'p_sel_k', 'p_sel_v'), (kc, vc, ks, vs)):
                st[n].append(kv4(a))
            st['p_win_k'].append(kv4(kw[seq - wb:]))
            st['p_win_v'].append(kv4(vw[seq - wb:]))
            q, kc, vc, ks, vs, kw, vw, gl, dp, dg = _col_splits(_bmajor(zs, nb), split_o)
            ck = nsa_compress_paged(cache_nsa_cmp_k[i], page_table, *phi_k)
            cv = nsa_compress_paged(cache_nsa_cmp_v[i], page_table, *phi_v)
            win_k, win_v = cache_nsa_win_k[i], cache_nsa_win_v[i]
            o_c = nsa_sample(q, gl, ck, cv, win_k, win_v, ks, vs, kw, vw,
                             cache_nsa_sel_k[i], cache_nsa_sel_v[i], page_table)
            o_d, hist = conformer_conv(_tmajor(dp), _tmajor(dg), conv_d_w[i], conv_d_b[i], ln_d_g[i],
                                       ln_d_b[i], _tmajor(state_conv_d[i]), tm=ts, shift=nb)
            st['s_conv_d'].append(_bmajor(hist, nb))
            ys = mm_norm_res([_tmajor(o_c), o_d], w_out_o[i], g[1], ys, tm=ts, tn=512)
            kv4 = lambda a: a.reshape(nb, nq, n_kv_c, HD)
            for n, a in zip(('s_cmp_k', 's_cmp_v', 's_sel_k', 's_sel_v'), (kc, vc, ks, vs)):
                st[n].append(kv4(a))
            st['s_win_k'].append(jnp.concatenate([win_k, kv4(kw)], axis=1)[:, nq:])
            st['s_win_v'].append(jnp.concatenate([win_v, kv4(vw)], axis=1)[:, nq:])

        n_mem = mem.shape[0]
        mk = norm_mm(mem, mem_norm_g[l], w_mk[l], tm=n_mem, tn=512)
        mv = norm_mm(mem, mem_norm_g[l], w_mv[l], tm=n_mem, tn=512)
        st['p_mem_k'].append(mk.reshape(1, n_mem, -1, HD))
        st['p_mem_v'].append(mv.reshape(1, n_mem, -1, HD))
        yp = cross_attn(yp, g[2], w_mq[l], mk, mv, w_mo[l], g[3], tm=tm_mid)
        qs = norm_mm(ys, g[2], w_mq[l], tm=ts, tn=512)
        cmk = cache_mem_k[l].reshape(nb, cache_mem_k.shape[2], -1)
        cmv = cache_mem_v[l].reshape(nb, cache_mem_v.shape[2], -1)
        a = mem_attn_batched(_bmajor(qs, nb), cmk, cmv)
        ys = mm_norm_res([_tmajor(a)], w_mo[l], g[3], ys, tm=ts, tn=512)

        kw_f = ffn_conv_w.shape[1]
        gact, hist = ffn_up(yp, g[4], w_up[l], ffn_conv_w[l], jnp.zeros((_halo_rows(kw_f, 1), d_ff), F32),
                            tm=tm_big, shift=1)
        st['p_ffn'].append(hist[hist.shape[0] - (kw_f - 1):][None])
        yp = mm_norm_res([gact], w_down[l], g[5], yp, tm=tm_mid, tn=256)
        gact, hist = ffn_up(ys, g[4], w_up[l], ffn_conv_w[l], _tmajor(state_ffn_conv[l]), tm=ts, shift=nb)
        st['s_ffn'].append(_bmajor(hist, nb))
        ys = mm_norm_res([gact], w_down[l], g[5], ys, tm=ts, tn=256)

    out = {n: jnp.stack(a) for n, a in st.items()}
    return (yp.reshape(1, seq, d), _bmajor(ys, nb)) + tuple(out[n] for n in names)
```

```python
import functools

import jax
import jax.numpy as jnp
from jax import lax
from jax.experimental import pallas as pl
from jax.experimental.pallas import tpu as pltpu

F32 = jnp.float32
BF16 = jnp.bfloat16
I32 = jnp.int32

EPS = 1e-6
NEG = -1e30
BIG = 1e4
HD = 128
PAGE = 128
CMP_BLK = 64
N_SEL = 16
WINDOW = 512
TOPK_MAX = 256
H_IDX = 16
D_IDX = 64
INT_MIN = -(2 ** 31)

VMEM_LIMIT_BYTES = 56 * 1024 * 1024


def _params(*sem):
    return pltpu.CompilerParams(dimension_semantics=sem, vmem_limit_bytes=VMEM_LIMIT_BYTES)


def _rms(x, g):
    return x * lax.rsqrt(jnp.mean(x * x, axis=-1, keepdims=True) + EPS) * g


def _dot(a, b):
    return jnp.dot(a.astype(BF16), b.astype(BF16), preferred_element_type=F32)


def _dot_nt(a, b):
    return lax.dot_general(a.astype(BF16), b.astype(BF16), (((1,), (1,)), ((), ())),
                           preferred_element_type=F32)


def _gain_spec(g, layer):
    return pl.BlockSpec((None,) + g.shape[1:], lambda *_: (layer, 0, 0))


def _norm_mm_kernel(x_ref, g_ref, w_ref, o_ref, xn_ref, *, gr):
    @pl.when(pl.program_id(1) == 0)
    def _():
        xn_ref[...] = _rms(x_ref[...], g_ref[gr:gr + 1, :]).astype(BF16)

    o_ref[...] = jnp.dot(xn_ref[...], w_ref[...].astype(BF16), preferred_element_type=F32)


def norm_mm(x, g, gl, gr, w, wl, *, tm, tn):
    m, k = x.shape
    n = w.shape[2]
    return pl.pallas_call(
        functools.partial(_norm_mm_kernel, gr=gr),
        grid=(m // tm, pl.cdiv(n, tn)),
        in_specs=[pl.BlockSpec((tm, k), lambda i, j: (i, 0)),
                  _gain_spec(g, gl),
                  pl.BlockSpec((None, k, tn), lambda i, j: (wl, 0, j))],
        out_specs=pl.BlockSpec((tm, tn), lambda i, j: (i, j)),
        out_shape=jax.ShapeDtypeStruct((m, n), F32),
        scratch_shapes=[pltpu.VMEM((tm, k), BF16)],
        compiler_params=_params("arbitrary", "arbitrary"),
        name="norm_mm",
    )(x, g, w)


def _mm_norm_res_kernel(*refs, n_in, widths, tn, nj, gr):
    a_refs = refs[:n_in]
    w_ref, g_ref, y_ref, o_ref, a_bf = refs[n_in:]
    j = pl.program_id(1)

    @pl.when(j == 0)
    def _():
        off = 0
        for a_ref, wd in zip(a_refs, widths):
            a_bf[:, off:off + wd] = a_ref[...].astype(BF16)
            off += wd

    col = pl.multiple_of(j * tn, tn)
    o_ref[:, pl.ds(col, tn)] = jnp.dot(a_bf[...], w_ref[...].astype(BF16),
                                       preferred_element_type=F32)

    @pl.when(j == nj - 1)
    def _():
        o_ref[...] = y_ref[...] + _rms(o_ref[...], g_ref[gr:gr + 1, :])


def mm_norm_res(a_list, w, wl, g, gl, gr, y, *, tm, tn):
    m, n = y.shape
    widths = tuple(a.shape[1] for a in a_list)
    k = sum(widths)
    assert w.shape[1:] == (k, n) and n % tn == 0
    nj = n // tn
    kern = functools.partial(_mm_norm_res_kernel, n_in=len(a_list), widths=widths, tn=tn, nj=nj,
                             gr=gr)
    return pl.pallas_call(
        kern,
        grid=(m // tm, nj),
        in_specs=[pl.BlockSpec((tm, wd), lambda i, j: (i, 0)) for wd in widths]
        + [pl.BlockSpec((None, k, tn), lambda i, j: (wl, 0, j)),
           _gain_spec(g, gl),
           pl.BlockSpec((tm, n), lambda i, j: (i, 0))],
        out_specs=pl.BlockSpec((tm, n), lambda i, j: (i, 0)),
        out_shape=jax.ShapeDtypeStruct((m, n), F32),
        scratch_shapes=[pltpu.VMEM((tm, k), BF16)],
        compiler_params=_params("arbitrary", "arbitrary"),
        name="mm_norm_res",
    )(*a_list, w, g, y)


def _order_key(x):
    b = pltpu.bitcast(x, I32)
    return b ^ ((b >> 31) & 0x7FFFFFFF)


def _kth_largest_key(count_ge, k, shape):
    t0 = jnp.where(count_ge(jnp.zeros(shape, I32)) >= k, 0, INT_MIN).astype(I32)

    def bit_body(n, t):
        cand = t | jnp.left_shift(jnp.int32(1), 30 - n)
        return jnp.where(count_ge(cand) >= k, cand, t)

    return lax.fori_loop(0, 31, bit_body, t0)


LOG2E = 1.4426950408889634
M_INIT = 0.1 * NEG


def _stage_queries(q_ref, qg_sc, n_groups, rep, tq):
    c = HD ** -0.5 * LOG2E
    for g in range(n_groups):
        for r in range(rep):
            h = g * rep + r
            qg_sc[g, r * tq:(r + 1) * tq, :] = (q_ref[:, h * HD:(h + 1) * HD] * c).astype(BF16)


def _flash_step(qg, kc, vc, bias, carry):
    m, l, acc = carry
    s = _dot_nt(qg, kc) + bias
    m_new = jnp.maximum(m, jnp.max(s, axis=1, keepdims=True))
    p = jnp.exp2(s - m_new)
    alpha = jnp.exp2(m - m_new)
    return (m_new, alpha * l + jnp.sum(p, axis=1, keepdims=True),
            alpha * acc + jnp.dot(p.astype(BF16), vc, preferred_element_type=F32))


def _flash_init(rows):
    return (jnp.full((rows, 1), M_INIT, F32), jnp.zeros((rows, 1), F32), jnp.zeros((rows, HD), F32))


def _dsa_prompt_kernel(q_ref, qia_ref, qib_ref, iw_ref, k_ref, v_ref, ik_ref, o_ref,
                       key_sc, qs_sc, wb_sc, qg_sc, *, tq, tks, tka, topk, n_heads, n_groups):
    i = pl.program_id(0)
    q0 = i * tq
    nch_a = (q0 + tq + tka - 1) // tka
    nch_s = nch_a * (tka // tks)
    rep = n_heads // n_groups
    qpos = q0 + lax.broadcasted_iota(I32, (tq, 1), 0)

    half = H_IDX // 2
    wscale = H_IDX ** -0.5 * D_IDX ** -0.5
    for h in range(H_IDX):
        src = qia_ref if h < half else qib_ref
        hh = h % half
        qs_sc[h * tq:(h + 1) * tq, :] = src[:, hh * D_IDX:(hh + 1) * D_IDX].astype(BF16)
        wb_sc[h] = jnp.broadcast_to(iw_ref[:, D_IDX + h:D_IDX + h + 1] * wscale, (tq, 128))
    _stage_queries(q_ref, qg_sc, n_groups, rep, tq)

    def score_body(c, carry):
        col = pl.multiple_of(c * tks, tks)
        ikc = ik_ref[pl.ds(col, tks), :]
        acc = jnp.zeros((tq, tks), F32)
        for h in range(H_IDX):
            s = lax.dot_general(qs_sc[h * tq:(h + 1) * tq, :], ikc, (((1,), (1,)), ((), ())),
                                preferred_element_type=F32)
            acc = acc + jnp.concatenate([wb_sc[h]] * (tks // 128), axis=1) * jnp.maximum(s, 0.0)
        kpos = col + lax.broadcasted_iota(I32, (1, tks), 1)
        key_sc[:, pl.ds(col, tks)] = jnp.where(kpos <= qpos, _order_key(acc), INT_MIN)
        return carry

    lax.fori_loop(0, nch_s, score_body, 0)

    def count_ge(t):
        tb = jnp.broadcast_to(t, (tq, 128))

        def body(c, cnt):
            col = pl.multiple_of(c * tka, tka)
            for j in range(tka // 128):
                cnt = cnt + jnp.where(key_sc[:, pl.ds(col + j * 128, 128)] >= tb, 1, 0)
            return cnt

        cnt = lax.fori_loop(0, nch_a, body, jnp.zeros((tq, 128), I32))
        return jnp.sum(cnt, axis=1, keepdims=True)

    thr = jnp.maximum(_kth_largest_key(count_ge, topk, (tq, 1)), INT_MIN + 1)

    def bias_body(c, carry):
        col = pl.multiple_of(c * tks, tks)
        bias = jnp.where(key_sc[:, pl.ds(col, tks)] >= thr, 0.0, NEG)
        key_sc[:, pl.ds(col, tks)] = pltpu.bitcast(bias, I32)
        return carry

    lax.fori_loop(0, nch_s, bias_body, 0)

    def att_body(c, carry):
        col = pl.multiple_of(c * tka, tka)
        bias = pltpu.bitcast(key_sc[:, pl.ds(col, tka)], F32)
        bias = jnp.concatenate([bias] * rep, axis=0)
        out = []
        for g in range(n_groups):
            kc = k_ref[pl.ds(col, tka), g * HD:(g + 1) * HD]
            vc = v_ref[pl.ds(col, tka), g * HD:(g + 1) * HD]
            out.append(_flash_step(qg_sc[g], kc, vc, bias, carry[g]))
        return tuple(out)

    res = lax.fori_loop(0, nch_a, att_body, tuple(_flash_init(rep * tq) for _ in range(n_groups)))
    for g in range(n_groups):
        _, l, acc = res[g]
        o = acc / jnp.where(l > 0, l, 1.0)
        for r in range(rep):
            o_ref[:, (g * rep + r) * HD:(g * rep + r + 1) * HD] = o[r * tq:(r + 1) * tq, :]


def dsa_prompt(z, q_off, qi_off, iw_off, n_heads, k_bf, v_bf, ik_bf, *, tq=128, tks=256, tka=512):
    t = z.shape[0]
    dq = n_heads * HD
    half = H_IDX * D_IDX // 2
    n_groups = k_bf.shape[1] // HD
    assert q_off % dq == 0 and qi_off % half == 0 and iw_off % 128 == 0 and D_IDX + H_IDX <= 128
    assert t % tka == 0 and tka % tks == 0
    topk = min(TOPK_MAX, t // 4)
    kern = functools.partial(_dsa_prompt_kernel, tq=tq, tks=tks, tka=tka, topk=topk,
                             n_heads=n_heads, n_groups=n_groups)
    full = lambda a: pl.BlockSpec(a.shape, lambda i: (0, 0))
    col = lambda w, j: pl.BlockSpec((tq, w), lambda i: (i, j))
    return pl.pallas_call(
        kern,
        grid=(t // tq,),
        in_specs=[col(dq, q_off // dq), col(half, qi_off // half), col(half, qi_off // half + 1),
                  col(128, iw_off // 128), full(k_bf), full(v_bf), full(ik_bf)],
        out_specs=pl.BlockSpec((tq, dq), lambda i: (i, 0)),
        out_shape=jax.ShapeDtypeStruct((t, dq), F32),
        scratch_shapes=[pltpu.VMEM((tq, t), I32), pltpu.VMEM((H_IDX * tq, D_IDX), BF16),
                        pltpu.VMEM((H_IDX, tq, 128), F32),
                        pltpu.VMEM((n_groups, n_heads // n_groups * tq, HD), BF16)],
        compiler_params=_params("arbitrary"),
        name="dsa_prompt",
    )(z, z, z, z, k_bf, v_bf, ik_bf)


def _nsa_compress_kernel(x_ref, pe_ref, w1_ref, w2_ref, o_ref, *, nblk, n_groups):
    _compress_rows(x_ref, pe_ref, w1_ref, w2_ref, o_ref, nblk=nblk, n_groups=n_groups)


def _compress_rows(x_ref, pe_ref, w1_ref, w2_ref, o_ref, *, nblk, n_groups):
    acc = jnp.zeros((n_groups * nblk, HD), F32)
    for p in range(CMP_BLK):
        pe_p = pe_ref[p:p + 1, :]
        lhs = jnp.concatenate(
            [x_ref[pl.ds(p * n_groups + g, nblk, stride=CMP_BLK * n_groups), :] + pe_p
             for g in range(n_groups)], axis=0)
        acc = acc + _dot(lhs, w1_ref[p * HD:(p + 1) * HD, :])
    out = _dot(jnp.maximum(acc, 0.0), w2_ref[...])
    for g in range(n_groups):
        o_ref[:, g * HD:(g + 1) * HD] = out[g * nblk:(g + 1) * nblk, :]


def nsa_compress(x, pe, w1, w2, n_groups):
    b, tg, _ = x.shape
    nblk = tg // (n_groups * CMP_BLK)
    kern = functools.partial(_nsa_compress_kernel, nblk=nblk, n_groups=n_groups)
    full = lambda a: pl.BlockSpec(a.shape, lambda i: (0,) * a.ndim)
    return pl.pallas_call(
        kern,
        grid=(b,),
        in_specs=[pl.BlockSpec((None, tg, HD), lambda i: (i, 0, 0)), full(pe), full(w1), full(w2)],
        out_specs=pl.BlockSpec((None, nblk, n_groups * HD), lambda i: (i, 0, 0)),
        out_shape=jax.ShapeDtypeStruct((b, nblk, n_groups * HD), F32),
        compiler_params=_params("arbitrary"),
        name="nsa_compress",
    )(x, pe, w1, w2)


def _masked_softmax(s, ok):
    s = jnp.where(ok, s, NEG)
    e = jnp.where(ok, jnp.exp(s - jnp.max(s, axis=-1, keepdims=True)), 0.0)
    d = jnp.sum(e, axis=-1, keepdims=True)
    return e / jnp.where(d > 0, d, 1.0)


def _top_blocks(score, n_sel):
    nb = score.shape[1]
    col = lax.broadcasted_iota(I32, score.shape, 1)
    sel = jnp.zeros(score.shape, F32)
    for _ in range(n_sel):
        m = jnp.max(score, axis=1, keepdims=True)
        idx = jnp.min(jnp.where(score == m, col, nb), axis=1, keepdims=True)
        hit = col == idx
        sel = jnp.where(hit, 1.0, sel)
        score = jnp.where(hit, -3e38, score)
    return sel


def _masked_softmax2(s, ok):
    s = jnp.where(ok, s, NEG)
    e = jnp.where(ok, jnp.exp2(s - jnp.max(s, axis=-1, keepdims=True)), 0.0)
    d = jnp.sum(e, axis=-1, keepdims=True)
    return e / jnp.where(d > 0, d, 1.0)


def _nsa_prompt_kernel(q_ref, gl_ref, ck_ref, cv_ref, ex_ref, ks_ref, vs_ref, kw_ref, vw_ref, o_ref,
                       qg_sc, bias_sc, *, tq, tk, n_heads, n_groups, nbc, gl_lane):
    i = pl.program_id(0)
    q0 = i * tq
    rep = n_heads // n_groups
    qpos = q0 + lax.broadcasted_iota(I32, (tq, 1), 0)
    jb = lax.broadcasted_iota(I32, (1, nbc), 1)
    cmp_ok = jnp.concatenate([jnp.where(((jb + 1) * CMP_BLK - 1) <= qpos, 1, 0)] * rep, axis=0) > 0
    cur = qpos // CMP_BLK
    gates = jax.nn.sigmoid(gl_ref[:, gl_lane:gl_lane + 3 * n_heads])

    _stage_queries(q_ref, qg_sc, n_groups, rep, tq)
    w0 = pl.multiple_of(jnp.maximum(q0 - WINDOW, 0), tq)
    wlen = WINDOW + tq
    diff = qpos - (w0 + lax.broadcasted_iota(I32, (1, wlen), 1))
    w_ok = jnp.where((diff >= 0) & (diff <= WINDOW), 1, 0)
    w_ok = jnp.concatenate([w_ok] * rep, axis=0) > 0
    imps = []
    for g in range(n_groups):
        p_c = _masked_softmax2(_dot_nt(qg_sc[g], ck_ref[g]), cmp_ok)
        o_c = _dot(p_c, cv_ref[g])
        imp = p_c[0:tq]
        for r in range(1, rep):
            imp = imp + p_c[r * tq:(r + 1) * tq]
        imps.append(imp)
        kw = kw_ref[pl.ds(w0, wlen), g * HD:(g + 1) * HD]
        vw = vw_ref[pl.ds(w0, wlen), g * HD:(g + 1) * HD]
        o_w = _dot(_masked_softmax2(_dot_nt(qg_sc[g], kw), w_ok), vw)
        for r in range(rep):
            h = g * rep + r
            rows = slice(r * tq, (r + 1) * tq)
            o_ref[:, h * HD:(h + 1) * HD] = (gates[:, 3 * h:3 * h + 1] * o_c[rows]
                                             + gates[:, 3 * h + 2:3 * h + 3] * o_w[rows])

    imp = jnp.concatenate(imps, axis=0)
    cur2 = jnp.concatenate([cur] * n_groups, axis=0)
    blk_score = jnp.where(jb > cur2, NEG, jnp.where((jb == cur2) | (jb == 0), BIG, imp))
    selm = _top_blocks(blk_score, min(N_SEL, nbc)).astype(BF16)

    nch = (q0 + tq + tk - 1) // tk

    def bias_body(c, carry):
        col = pl.multiple_of(c * tk, tk)
        picked = jnp.dot(selm, ex_ref[:, pl.ds(col, tk)], preferred_element_type=F32)
        kpos = col + lax.broadcasted_iota(I32, (1, tk), 1)
        for g in range(n_groups):
            ok = (picked[g * tq:(g + 1) * tq] > 0.5) & (kpos <= qpos)
            bias_sc[g, :, pl.ds(col, tk)] = jnp.where(ok, 0.0, NEG)
        return carry

    lax.fori_loop(0, nch, bias_body, 0)

    def att_body(c, carry):
        col = pl.multiple_of(c * tk, tk)
        out = []
        for g in range(n_groups):
            bias = jnp.concatenate([bias_sc[g, :, pl.ds(col, tk)]] * rep, axis=0)
            kc = ks_ref[pl.ds(col, tk), g * HD:(g + 1) * HD]
            vc = vs_ref[pl.ds(col, tk), g * HD:(g + 1) * HD]
            out.append(_flash_step(qg_sc[g], kc, vc, bias, carry[g]))
        return tuple(out)

    res = lax.fori_loop(0, nch, att_body, tuple(_flash_init(rep * tq) for _ in range(n_groups)))

    for g in range(n_groups):
        _, l, acc = res[g]
        o_s = acc / jnp.where(l > 0, l, 1.0)
        for r in range(rep):
            h = g * rep + r
            o_ref[:, h * HD:(h + 1) * HD] += gates[:, 3 * h + 1:3 * h + 2] * o_s[r * tq:(r + 1) * tq]


def nsa_prompt(z, q_off, gl_off, n_heads, ck_bf, cv_bf, ks_bf, vs_bf, kw_bf, vw_bf, *, tq=128, tk=512):
    t = z.shape[0]
    dq = n_heads * HD
    n_groups, nbc, _ = ck_bf.shape
    gl_lane = gl_off % 128
    assert t >= WINDOW + tq and t % tk == 0 and nbc == t // CMP_BLK
    assert q_off % dq == 0 and gl_lane + 3 * n_heads <= 128
    kern = functools.partial(_nsa_prompt_kernel, tq=tq, tk=tk, n_heads=n_heads,
                             n_groups=n_groups, nbc=nbc, gl_lane=gl_lane)
    expand = (jnp.arange(t, dtype=I32)[None, :] // CMP_BLK
              == jnp.arange(nbc, dtype=I32)[:, None]).astype(BF16)
    full = lambda a: pl.BlockSpec(a.shape, lambda i: (0,) * a.ndim)
    col = lambda w, j: pl.BlockSpec((tq, w), lambda i: (i, j))
    return pl.pallas_call(
        kern,
        grid=(t // tq,),
        in_specs=[col(dq, q_off // dq), col(128, gl_off // 128), full(ck_bf), full(cv_bf),
                  full(expand), full(ks_bf), full(vs_bf), full(kw_bf), full(vw_bf)],
        out_specs=pl.BlockSpec((tq, dq), lambda i: (i, 0)),
        out_shape=jax.ShapeDtypeStruct((t, dq), F32),
        scratch_shapes=[pltpu.VMEM((n_groups, n_heads // n_groups * tq, HD), BF16),
                        pltpu.VMEM((n_groups, tq, t), F32)],
        compiler_params=_params("arbitrary"),
        name="nsa_prompt",
    )(z, z, ck_bf, cv_bf, expand, ks_bf, vs_bf, kw_bf, vw_bf)


def _halo_rows(kw, shift):
    return max(8, (kw - 1) * shift)


def _dwconv(ext_ref, w_ref, u, prev_ref, first, *, tm, kw, shift):
    hp = _halo_rows(kw, shift)

    @pl.when(first)
    def _():
        ext_ref[0:hp, :] = prev_ref[...]

    ext_ref[hp:hp + tm, :] = u
    y = None
    for i in range(kw):
        term = w_ref[i:i + 1, :] * ext_ref[pl.ds(hp - (kw - 1 - i) * shift, tm), :]
        y = term if y is None else y + term
    tail = ext_ref[tm:tm + hp, :]
    ext_ref[0:hp, :] = tail
    return y, tail


def _gated_conv_kernel(xa_ref, bg_ref, cg_ref, w_ref, prev_ref, o_ref, st_ref, ext_ref,
                       *, tm, kw, shift):
    u = cg_ref[...] * xa_ref[...]
    y, tail = _dwconv(ext_ref, w_ref, u, prev_ref, pl.program_id(0) == 0, tm=tm, kw=kw, shift=shift)
    o_ref[...] = bg_ref[...] * y
    st_ref[...] = tail


def gated_conv(z, c, w, prev, *, tm, shift):
    m = z.shape[0]
    kw = w.shape[0]
    hp = _halo_rows(kw, shift)
    kern = functools.partial(_gated_conv_kernel, tm=tm, kw=kw, shift=shift)
    col = lambda j: pl.BlockSpec((tm, c), lambda i: (i, j))
    return pl.pallas_call(
        kern,
        grid=(m // tm,),
        in_specs=[col(0), col(1), col(2), pl.BlockSpec((kw, c), lambda i: (0, 0)),
                  pl.BlockSpec((hp, c), lambda i: (0, 0))],
        out_specs=[pl.BlockSpec((tm, c), lambda i: (i, 0)), pl.BlockSpec((hp, c), lambda i: (0, 0))],
        out_shape=[jax.ShapeDtypeStruct((m, c), F32), jax.ShapeDtypeStruct((hp, c), F32)],
        scratch_shapes=[pltpu.VMEM((hp + tm, c), F32)],
        compiler_params=_params("arbitrary"),
        name="gated_conv",
    )(z, z, z, w, prev)


def _conformer_kernel(dpa_ref, dpb_ref, dga_ref, dgb_ref, w_ref, b_ref, lg_ref, lb_ref, prev_ref,
                      o_ref, st_ref, ext_ref, *, tm, kw, shift):
    u = jnp.concatenate([dpa_ref[...] * jax.nn.sigmoid(dga_ref[...]),
                         dpb_ref[...] * jax.nn.sigmoid(dgb_ref[...])], axis=1)
    c, tail = _dwconv(ext_ref, w_ref, u, prev_ref, pl.program_id(0) == 0, tm=tm, kw=kw, shift=shift)
    c = c + b_ref[...]
    mu = jnp.mean(c, axis=-1, keepdims=True)
    xc = c - mu
    y = xc * lax.rsqrt(jnp.mean(xc * xc, axis=-1, keepdims=True) + EPS) * lg_ref[...] + lb_ref[...]
    o_ref[...] = y * jax.nn.sigmoid(y)
    st_ref[...] = tail


def conformer_conv(z, dp_off, dg_off, w, b, ln_g, ln_b, prev, *, tm, shift):
    m = z.shape[0]
    kw, c = w.shape
    half = c // 2
    assert dp_off % half == 0 and dg_off % half == 0
    hp = _halo_rows(kw, shift)
    kern = functools.partial(_conformer_kernel, tm=tm, kw=kw, shift=shift)
    col = lambda off, j: pl.BlockSpec((tm, half), lambda i: (i, off // half + j))
    row = pl.BlockSpec((tm, c), lambda i: (i, 0))
    vec = pl.BlockSpec((1, c), lambda i: (0, 0))
    return pl.pallas_call(
        kern,
        grid=(m // tm,),
        in_specs=[col(dp_off, 0), col(dp_off, 1), col(dg_off, 0), col(dg_off, 1),
                  pl.BlockSpec((kw, c), lambda i: (0, 0)), vec, vec, vec,
                  pl.BlockSpec((hp, c), lambda i: (0, 0))],
        out_specs=[row, pl.BlockSpec((hp, c), lambda i: (0, 0))],
        out_shape=[jax.ShapeDtypeStruct((m, c), F32), jax.ShapeDtypeStruct((hp, c), F32)],
        scratch_shapes=[pltpu.VMEM((hp + tm, c), F32)],
        compiler_params=_params("arbitrary"),
        name="conformer_conv",
    )(z, z, z, z, w, b.reshape(1, c), ln_g.reshape(1, c), ln_b.reshape(1, c), prev)


FFN_TF = 128


def _ffn_up_kernel(y_ref, g_ref, wa_ref, wv_ref, cw_ref, prev_ref, o_ref, st_ref,
                   xn_ref, wcat_ref, halo_ref, ext_ref, *, tm, kw, shift, gr):
    i = pl.program_id(0)
    j = pl.program_id(1)
    hp = _halo_rows(kw, shift)

    @pl.when(j == 0)
    def _():
        xn_ref[...] = _rms(y_ref[...], g_ref[gr:gr + 1, :]).astype(BF16)

    wcat_ref[:, 0:FFN_TF] = wa_ref[...].astype(BF16)
    wcat_ref[:, FFN_TF:2 * FFN_TF] = wv_ref[...].astype(BF16)
    h = jnp.dot(xn_ref[...], wcat_ref[...], preferred_element_type=F32)
    a = h[:, 0:FFN_TF]
    v = h[:, FFN_TF:2 * FFN_TF]

    @pl.when(i > 0)
    def _():
        ext_ref[0:hp, :] = halo_ref[j]

    c, tail = _dwconv(ext_ref, cw_ref, a, prev_ref, i == 0, tm=tm, kw=kw, shift=shift)
    halo_ref[j] = tail
    st_ref[...] = tail
    o_ref[...] = (c * jax.nn.sigmoid(c) * v).astype(BF16)


def ffn_up(y, g, gl, gr, w_up, wl, conv_w, prev, *, tm, shift):
    m, d = y.shape
    kw, f = conv_w.shape
    nf = f // FFN_TF
    assert f % FFN_TF == 0 and w_up.shape[1:] == (d, 2 * f)
    hp = _halo_rows(kw, shift)
    nm = m // tm
    kern = functools.partial(_ffn_up_kernel, tm=tm, kw=kw, shift=shift, gr=gr)
    return pl.pallas_call(
        kern,
        grid=(nm, nf),
        in_specs=[pl.BlockSpec((tm, d), lambda i, j: (i, 0)),
                  _gain_spec(g, gl),
                  pl.BlockSpec((None, d, FFN_TF), lambda i, j: (wl, 0, j)),
                  pl.BlockSpec((None, d, FFN_TF), lambda i, j: (wl, 0, nf + j)),
                  pl.BlockSpec((kw, FFN_TF), lambda i, j: (0, j)),
                  pl.BlockSpec((hp, FFN_TF), lambda i, j: (0, j))],
        out_specs=[pl.BlockSpec((tm, FFN_TF), lambda i, j: (i, j)),
                   pl.BlockSpec((hp, FFN_TF), lambda i, j: (i, j))],
        out_shape=[jax.ShapeDtypeStruct((m, f), BF16), jax.ShapeDtypeStruct((nm * hp, f), F32)],
        scratch_shapes=[pltpu.VMEM((tm, d), BF16), pltpu.VMEM((d, 2 * FFN_TF), BF16),
                        pltpu.VMEM((nf, hp, FFN_TF), F32), pltpu.VMEM((hp + tm, FFN_TF), F32)],
        compiler_params=_params("arbitrary", "arbitrary"),
        name="ffn_up",
    )(y, g, w_up, w_up, conv_w, prev)


def _mem_heads(q, mk_head, mv_head, n_heads):
    scale = HD ** -0.5
    outs = []
    for h in range(n_heads):
        s = _dot_nt(q[:, h * HD:(h + 1) * HD], mk_head(h)) * scale
        e = jnp.exp(s - jnp.max(s, axis=-1, keepdims=True))
        p = e / jnp.sum(e, axis=-1, keepdims=True)
        outs.append(_dot(p, mv_head(h)))
    return jnp.concatenate(outs, axis=1)


def _xattn_kernel(y_ref, g_ref, wq_ref, mk_ref, mv_ref, wo_ref, o_ref, wq_bf, wo_bf,
                  *, n_heads, gr_in, gr_out):
    @pl.when(pl.program_id(0) == 0)
    def _():
        wq_bf[...] = wq_ref[...].astype(BF16)
        wo_bf[...] = wo_ref[...].astype(BF16)

    y = y_ref[...]
    q = jnp.dot(_rms(y, g_ref[gr_in:gr_in + 1, :]).astype(BF16), wq_bf[...],
                preferred_element_type=F32)
    o = _mem_heads(q, lambda h: mk_ref[:, h * HD:(h + 1) * HD],
                   lambda h: mv_ref[:, h * HD:(h + 1) * HD], n_heads)
    f = jnp.dot(o.astype(BF16), wo_bf[...], preferred_element_type=F32)
    o_ref[...] = y + _rms(f, g_ref[gr_out:gr_out + 1, :])


def cross_attn(y, g, gl, gr_in, gr_out, wq, wo, wl, mk, mv, *, tm):
    m, d = y.shape
    dh = wq.shape[2]
    kern = functools.partial(_xattn_kernel, n_heads=dh // HD, gr_in=gr_in, gr_out=gr_out)
    full = lambda a: pl.BlockSpec(a.shape, lambda i: (0, 0))
    layer = lambda a: pl.BlockSpec((None,) + a.shape[1:], lambda i: (wl, 0, 0))
    row = pl.BlockSpec((tm, d), lambda i: (i, 0))
    return pl.pallas_call(
        kern,
        grid=(m // tm,),
        in_specs=[row, _gain_spec(g, gl), layer(wq), full(mk), full(mv), layer(wo)],
        out_specs=row,
        out_shape=jax.ShapeDtypeStruct((m, d), F32),
        scratch_shapes=[pltpu.VMEM(wq.shape[1:], BF16), pltpu.VMEM(wo.shape[1:], BF16)],
        compiler_params=_params("arbitrary"),
        name="cross_attn",
    )(y, g, wq, mk, mv, wo)


def _mem_attn_kernel(q_ref, mk_ref, mv_ref, o_ref, *, n_heads):
    o_ref[...] = _mem_heads(q_ref[...], lambda h: mk_ref[:, h, :], lambda h: mv_ref[:, h, :], n_heads)


def mem_attn_batched(q, mk, mv, layer):
    b, t, dh = q.shape
    kern = functools.partial(_mem_attn_kernel, n_heads=dh // HD)
    cache = pl.BlockSpec((None, None) + mk.shape[2:], lambda i: (layer, i, 0, 0, 0))
    blk = pl.BlockSpec((None, t, dh), lambda i: (i, 0, 0))
    return pl.pallas_call(
        kern,
        grid=(b,),
        in_specs=[blk, cache, cache],
        out_specs=blk,
        out_shape=jax.ShapeDtypeStruct(q.shape, F32),
        compiler_params=_params("arbitrary"),
        name="mem_attn_batched",
    )(q, mk, mv)


def _page_copy(cache, pt_ref, b, p, buf, slot, sem):
    rows = cache.shape[1]
    return pltpu.make_async_copy(cache.at[pt_ref[b, p]],
                                 buf.at[slot, pl.ds(p * rows, rows), :], sem.at[slot])


def _for_pages(caches, bufs, sems, pt_ref, b, slot, n_pages, fn):
    def body(p, carry):
        for cache, buf, sem in zip(caches, bufs, sems):
            fn(_page_copy(cache, pt_ref, b, p, buf, slot, sem))
        return carry

    lax.fori_loop(0, n_pages, body, 0)


def _fetch_pages(caches, bufs, sems, pt_ref, n_pages):
    b = pl.program_id(0)
    slot = b % 2
    args = (caches, bufs, sems, pt_ref)

    @pl.when(b == 0)
    def _():
        _for_pages(*args, b, slot, n_pages, lambda cp: cp.start())

    @pl.when(b + 1 < pl.num_programs(0))
    def _():
        _for_pages(*args, b + 1, 1 - slot, n_pages, lambda cp: cp.start())

    _for_pages(*args, b, slot, n_pages, lambda cp: cp.wait())
    return slot


def _paged_call(kern, pt, inputs, n_any, out_shape, out_block, scratch, name):
    nb = pt.shape[0]

    def spec(a):
        nd = a.ndim - 1
        return pl.BlockSpec((None,) + a.shape[1:], lambda i, pt_ref: (i,) + (0,) * nd)

    def shared(a):
        nd = a.ndim
        return pl.BlockSpec(a.shape, lambda i, pt_ref: (0,) * nd)

    in_specs = []
    for a in inputs[:len(inputs) - n_any]:
        in_specs.append(spec(a) if a.shape[0] == nb and a.ndim >= 3 else shared(a))
    in_specs += [pl.BlockSpec(memory_space=pl.ANY)] * n_any
    nd_o = len(out_block)
    return pl.pallas_call(
        kern,
        grid_spec=pltpu.PrefetchScalarGridSpec(
            num_scalar_prefetch=1,
            grid=(nb,),
            in_specs=in_specs,
            out_specs=pl.BlockSpec((None,) + tuple(out_block), lambda i, pt_ref: (i,) + (0,) * nd_o),
            scratch_shapes=scratch),
        out_shape=jax.ShapeDtypeStruct((nb,) + tuple(out_block), out_shape),
        compiler_params=_params("arbitrary"),
        name=name,
    )(pt, *inputs)


def _attend_two_parts(qg, k_a, v_a, ok_a, k_b, v_b, ok_b):
    scale = HD ** -0.5
    s_a = jnp.where(ok_a, _dot_nt(qg, k_a) * scale, NEG)
    s_b = jnp.where(ok_b, _dot_nt(qg, k_b) * scale, NEG)
    mx = jnp.maximum(jnp.max(s_a, axis=1, keepdims=True), jnp.max(s_b, axis=1, keepdims=True))
    e_a = jnp.where(ok_a, jnp.exp(s_a - mx), 0.0)
    e_b = jnp.where(ok_b, jnp.exp(s_b - mx), 0.0)
    den = jnp.sum(e_a, axis=1, keepdims=True) + jnp.sum(e_b, axis=1, keepdims=True)
    return (_dot(e_a, v_a) + _dot(e_b, v_b)) / jnp.where(den > 0, den, 1.0)


def _dsa_sample_select_kernel(pt_ref, qi_ref, wi_ref, ikn_ref, ik_hbm, m_ref, ikbuf, key_sc, sem,
                              *, n_pages, tk, topk, n_new, nq, rows):
    slot = _fetch_pages([ik_hbm], [ikbuf], [sem], pt_ref, n_pages)
    past = n_pages * PAGE
    qi = qi_ref[...].astype(BF16)
    w = wi_ref[...] * (H_IDX ** -0.5 * D_IDX ** -0.5)

    def scores(ikc):
        r = jnp.maximum(_dot_nt(qi, ikc), 0.0) * w
        return jnp.sum(r.reshape(H_IDX, rows, ikc.shape[0]), axis=0)

    def score_body(c, carry):
        col = pl.multiple_of(c * tk, tk)
        key_sc[:, pl.ds(col, tk)] = _order_key(scores(ikbuf[slot, pl.ds(col, tk), :]))
        return carry

    lax.fori_loop(0, past // tk, score_body, 0)
    t_row = lax.rem(lax.broadcasted_iota(I32, (rows, PAGE), 0), nq)
    col = lax.broadcasted_iota(I32, (rows, PAGE), 1)
    new_ok = (col < n_new) & (col <= t_row)
    key_sc[:, past:past + PAGE] = jnp.where(new_ok, _order_key(scores(ikn_ref[...])), INT_MIN)

    def count_ge(t):
        return jnp.sum(jnp.where(key_sc[...] >= t, 1, 0), axis=1, keepdims=True)

    thr = jnp.maximum(_kth_largest_key(count_ge, topk, (rows, 1)), INT_MIN + 1)
    m_ref[...] = jnp.where(key_sc[...] >= thr, 1.0, 0.0)


def _dsa_sample_attn_kernel(pt_ref, q_ref, m_ref, kn_ref, vn_ref, k_hbm, v_hbm, o_ref,
                            kbuf, vbuf, ksem, vsem, *, n_pages, n_groups):
    slot = _fetch_pages([k_hbm, v_hbm], [kbuf, vbuf], [ksem, vsem], pt_ref, n_pages)
    past = n_pages * PAGE
    ok_c = m_ref[:, 0:past] > 0.5
    ok_n = m_ref[:, past:past + PAGE] > 0.5
    for g in range(n_groups):
        o_ref[g] = _attend_two_parts(
            q_ref[g].astype(BF16),
            kbuf[slot, pl.ds(g, past, stride=n_groups), :],
            vbuf[slot, pl.ds(g, past, stride=n_groups), :], ok_c,
            kn_ref[pl.ds(g, PAGE, stride=n_groups), :],
            vn_ref[pl.ds(g, PAGE, stride=n_groups), :], ok_n)


def _group_rows(x, n_groups):
    b, t, dq = x.shape
    rep = dq // HD // n_groups
    return x.reshape(b, t, n_groups, rep, HD).transpose(0, 2, 3, 1, 4).reshape(b, n_groups, rep * t, HD)


def _ungroup_rows(o, t):
    b, g, rows, _ = o.shape
    rep = rows // t
    return o.reshape(b, g, rep, t, HD).transpose(0, 3, 1, 2, 4).reshape(b, t, g * rep * HD)


def _pad_new_rows(x, n_groups):
    b, t, _ = x.shape
    x = x.reshape(b, t * n_groups, HD)
    return jnp.pad(x, ((0, 0), (0, (PAGE - t) * n_groups), (0, 0)))


def dsa_sample(q, qi, wi, k_new, v_new, ik_new, cache_k, cache_v, cache_ik, pt, *, tk=1024):
    nb, nq, dq = q.shape
    n_pages = pt.shape[1]
    n_groups = cache_k.shape[2]
    rep = dq // HD // n_groups
    rows = rep * nq
    past = n_pages * PAGE
    topk = min(TOPK_MAX, (past + nq) // 4)
    width = past + PAGE
    qi_r = jnp.broadcast_to(qi.reshape(nb, nq, H_IDX, D_IDX).transpose(0, 2, 1, 3)[:, :, None],
                            (nb, H_IDX, rep, nq, D_IDX)).reshape(nb, H_IDX * rows, D_IDX)
    wi_r = jnp.broadcast_to(wi.transpose(0, 2, 1)[:, :, None], (nb, H_IDX, rep, nq)).reshape(nb, H_IDX * rows, 1)
    ikn = jnp.pad(ik_new, ((0, 0), (0, PAGE - nq), (0, 0)))
    sel_kern = functools.partial(_dsa_sample_select_kernel, n_pages=n_pages, tk=tk, topk=topk,
                                 n_new=nq, nq=nq, rows=rows)
    mask = _paged_call(
        sel_kern, pt, [qi_r, wi_r, ikn, cache_ik], 1, F32, (rows, width),
        [pltpu.VMEM((2, past, D_IDX), F32), pltpu.VMEM((rows, width), I32),
         pltpu.SemaphoreType.DMA((2,))], "dsa_sample_select")
    ck = cache_k.reshape(cache_k.shape[0], PAGE * n_groups, HD)
    cv = cache_v.reshape(cache_v.shape[0], PAGE * n_groups, HD)
    att_kern = functools.partial(_dsa_sample_attn_kernel, n_pages=n_pages, n_groups=n_groups)
    o = _paged_call(
        att_kern, pt, [_group_rows(q, n_groups), mask, _pad_new_rows(k_new, n_groups),
                       _pad_new_rows(v_new, n_groups), ck, cv], 2, F32, (n_groups, rows, HD),
        [pltpu.VMEM((2, past * n_groups, HD), F32), pltpu.VMEM((2, past * n_groups, HD), F32),
         pltpu.SemaphoreType.DMA((2,)), pltpu.SemaphoreType.DMA((2,))], "dsa_sample_attn")
    return _ungroup_rows(o, nq)


def _nsa_compress_paged_kernel(pt_ref, pe_ref, w1_ref, w2_ref, x_hbm, o_ref, buf, sem,
                               *, n_pages, n_groups):
    slot = _fetch_pages([x_hbm], [buf], [sem], pt_ref, n_pages)
    _compress_rows(buf.at[slot], pe_ref, w1_ref, w2_ref, o_ref,
                   nblk=n_pages * PAGE // CMP_BLK, n_groups=n_groups)


def nsa_compress_paged(cache, pt, pe, w1, w2):
    n_groups = cache.shape[2]
    n_pages = pt.shape[1]
    past = n_pages * PAGE
    kern = functools.partial(_nsa_compress_paged_kernel, n_pages=n_pages, n_groups=n_groups)
    view = cache.reshape(cache.shape[0], PAGE * n_groups, HD)
    return _paged_call(
        kern, pt, [pe, w1.astype(BF16), w2, view], 1, F32, (past // CMP_BLK, n_groups * HD),
        [pltpu.VMEM((2, past * n_groups, HD), F32), pltpu.SemaphoreType.DMA((2,))],
        "nsa_compress_paged")


def _nsa_sample_kernel(pt_ref, q_ref, gl_ref, ck_ref, cv_ref, ex_ref, wk_ref, wv_ref, skn_ref, svn_ref,
                       wkn_ref, wvn_ref, ks_hbm, vs_hbm, o_ref, kbuf, vbuf, ksem, vsem,
                       *, n_pages, n_groups, nq, n_new):
    slot = _fetch_pages([ks_hbm, vs_hbm], [kbuf, vbuf], [ksem, vsem], pt_ref, n_pages)
    past = n_pages * PAGE
    scale = HD ** -0.5
    rows = q_ref.shape[1]
    rep = rows // nq
    nbc = ck_ref.shape[0]
    wlen = wk_ref.shape[0] // n_groups
    t_row = lax.rem(lax.broadcasted_iota(I32, (rows, 1), 0), nq)
    qpos = past + t_row
    jb = lax.broadcasted_iota(I32, (1, nbc), 1)
    cmp_ok = ((jb + 1) * CMP_BLK - 1) <= qpos
    cur = qpos // CMP_BLK
    newcol = lax.broadcasted_iota(I32, (1, PAGE), 1)
    new_ok = (newcol < n_new) & (newcol <= t_row)
    diff = qpos - (past - wlen + lax.broadcasted_iota(I32, (1, wlen), 1))
    win_ok = (diff >= 0) & (diff <= WINDOW)
    o_cs, scores = [], []
    for g in range(n_groups):
        cols = slice(g * HD, (g + 1) * HD)
        p_c = _masked_softmax(_dot_nt(q_ref[g].astype(BF16), ck_ref[:, cols]) * scale, cmp_ok)
        o_cs.append(_dot(p_c, cv_ref[:, cols]))
        imp = p_c
        for r in range(1, rep):
            imp = imp + pltpu.roll(p_c, r * nq, 0)
        scores.append(jnp.where(jb > cur, NEG, jnp.where((jb == cur) | (jb == 0), BIG, imp)))
    selm = _top_blocks(jnp.concatenate(scores, axis=0), N_SEL - 1).astype(BF16)
    picked_all = jnp.dot(selm, ex_ref[...], preferred_element_type=F32)
    for g in range(n_groups):
        qg = q_ref[g].astype(BF16)
        gates = jax.nn.sigmoid(gl_ref[g])
        o_c = o_cs[g]
        picked = picked_all[g * rows:(g + 1) * rows] > 0.5
        o_s = _attend_two_parts(
            qg, kbuf[slot, pl.ds(g, past, stride=n_groups), :],
            vbuf[slot, pl.ds(g, past, stride=n_groups), :], picked,
            skn_ref[pl.ds(g, PAGE, stride=n_groups), :],
            svn_ref[pl.ds(g, PAGE, stride=n_groups), :], new_ok)
        o_w = _attend_two_parts(
            qg, wk_ref[pl.ds(g, wlen, stride=n_groups), :],
            wv_ref[pl.ds(g, wlen, stride=n_groups), :], win_ok,
            wkn_ref[pl.ds(g, PAGE, stride=n_groups), :],
            wvn_ref[pl.ds(g, PAGE, stride=n_groups), :], new_ok)
        o_ref[g] = gates[:, 0:1] * o_c + gates[:, 1:2] * o_s + gates[:, 2:3] * o_w


def nsa_sample(q, gl, ck, cv, win_k, win_v, ks_new, vs_new, kw_new, vw_new, cache_ks, cache_vs, pt):
    nb, nq, dq = q.shape
    n_pages = pt.shape[1]
    n_groups = cache_ks.shape[2]
    n_heads = dq // HD
    rep = n_heads // n_groups
    rows = rep * nq
    past = n_pages * PAGE
    assert past % CMP_BLK == 0 and nq <= CMP_BLK and ck.shape[1] == past // CMP_BLK
    assert win_k.shape[1] == WINDOW
    glr = gl.reshape(nb, nq, n_groups, rep, 3).transpose(0, 2, 3, 1, 4).reshape(nb, n_groups, rows, 3)
    view = lambda c: c.reshape(c.shape[0], c.shape[1] * n_groups, HD)
    kern = functools.partial(_nsa_sample_kernel, n_pages=n_pages, n_groups=n_groups,
                             nq=nq, n_new=nq)
    pad = lambda x: _pad_new_rows(x, n_groups)
    expand = (jnp.arange(past, dtype=I32)[None, :] // CMP_BLK
              == jnp.arange(past // CMP_BLK, dtype=I32)[:, None]).astype(BF16)
    o = _paged_call(
        kern, pt, [_group_rows(q, n_groups), glr, ck, cv, expand, view(win_k), view(win_v),
                   pad(ks_new), pad(vs_new), pad(kw_new), pad(vw_new), view(cache_ks), view(cache_vs)],
        2, F32, (n_groups, rows, HD),
        [pltpu.VMEM((2, past * n_groups, HD), F32), pltpu.VMEM((2, past * n_groups, HD), F32),
         pltpu.SemaphoreType.DMA((2,)), pltpu.SemaphoreType.DMA((2,))], "nsa_sample")
    return _ungroup_rows(o, nq)


def _tmajor(a):
    b, t, c = a.shape
    return jnp.swapaxes(a, 0, 1).reshape(t * b, c)


def _bmajor(a, b):
    tb, c = a.shape
    return jnp.swapaxes(a.reshape(tb // b, b, c), 0, 1)


def _col_splits(z, sizes):
    out, off = [], 0
    for s in sizes:
        out.append(z[..., off:off + s])
        off += s
    return out


def _row_tile(m, pref):
    return pref if m % pref == 0 else m


def kernel(x_prompt, x_sample, state_conv_a, cache_dsa_k, cache_dsa_v, cache_dsa_ik, cache_nsa_cmp_k, cache_nsa_cmp_v, cache_nsa_sel_k, cache_nsa_sel_v, cache_nsa_win_k, cache_nsa_win_v, state_conv_d, state_ffn_conv, cache_mem_k, cache_mem_v, page_table, mem_prompt, w_in_e, conv_a_w, w_out_e, w_in_o, nsa_pe_k, nsa_w1_k, nsa_w2_k, nsa_pe_v, nsa_w1_v, nsa_w2_v, conv_d_w, conv_d_b, ln_d_g, ln_d_b, w_out_o, norm_g, mem_norm_g, w_mq, w_mk, w_mv, w_mo, w_up, ffn_conv_w, w_down):
    bp, seq, d = x_prompt.shape
    nb, nq, _ = x_sample.shape
    assert bp == 1
    depth = norm_g.shape[0]
    ts = nb * nq
    d_a = conv_a_w.shape[-1]
    d_d = conv_d_w.shape[-1]
    d_ff = ffn_conv_w.shape[-1]
    kv_b = cache_dsa_k.shape[-2] * HD
    kv_c = cache_nsa_cmp_k.shape[-2] * HD
    n_kv_c = cache_nsa_cmp_k.shape[-2]
    dq_b = w_out_e.shape[1] - d_a
    dq_c = w_out_o.shape[1] - d_d
    n_gate = 3 * dq_c // HD
    split_e = (d_a, d_a, d_a, dq_b, kv_b, kv_b, H_IDX * D_IDX, D_IDX, H_IDX)
    split_o = (dq_c,) + (kv_c,) * 6 + (d_d, d_d, n_gate)
    gl_src = dq_c + 6 * kv_c
    w_in_o = jnp.concatenate([w_in_o[:, :, :gl_src], w_in_o[:, :, gl_src + n_gate:],
                              w_in_o[:, :, gl_src:gl_src + n_gate]], axis=2)
    off_e = [sum(split_e[:j]) for j in range(len(split_e))]
    off_o = [sum(split_o[:j]) for j in range(len(split_o))]
    mem_g = mem_norm_g.reshape(depth, 1, d)
    tm_big = _row_tile(seq, 1024)
    tm_mid = _row_tile(seq, 512)
    wb = min(WINDOW, seq)

    yp = x_prompt.reshape(seq, d)
    ys = _tmajor(x_sample)
    mem = mem_prompt.reshape(mem_prompt.shape[1], d)
    names = ('p_conv_a', 'p_dsa_k', 'p_dsa_v', 'p_dsa_ik', 'p_cmp_k', 'p_cmp_v', 'p_sel_k', 'p_sel_v',
             'p_win_k', 'p_win_v', 'p_conv_d', 'p_ffn', 'p_mem_k', 'p_mem_v',
             's_conv_a', 's_dsa_k', 's_dsa_v', 's_dsa_ik', 's_cmp_k', 's_cmp_v', 's_sel_k', 's_sel_v',
             's_win_k', 's_win_v', 's_conv_d', 's_ffn')
    st = {n: [] for n in names}

    for l in range(depth):
        g = norm_g
        i = l // 2
        if l % 2 == 0:
            zp = norm_mm(yp, g, l, 0, w_in_e, i, tm=tm_big, tn=512)
            zs = norm_mm(ys, g, l, 0, w_in_e, i, tm=ts, tn=512)
            kw_a = conv_a_w.shape[1]
            o_a, hist = gated_conv(zp, d_a, conv_a_w[i], jnp.zeros((_halo_rows(kw_a, 1), d_a), F32),
                                   tm=tm_mid, shift=1)
            st['p_conv_a'].append(hist[hist.shape[0] - (kw_a - 1):][None])
            k, v, ik = lax.optimization_barrier(
                tuple(zp[:, off_e[j]:off_e[j] + split_e[j]] for j in (4, 5, 7)))
            o_b = dsa_prompt(zp, off_e[3], off_e[6], off_e[7], dq_b // HD,
                             k.astype(BF16), v.astype(BF16), ik.astype(BF16))
            yp = mm_norm_res([o_a, o_b], w_out_e, i, g, l, 1, yp, tm=tm_mid, tn=512)
            st['p_dsa_k'].append(k.reshape(1, seq, -1, HD))
            st['p_dsa_v'].append(v.reshape(1, seq, -1, HD))
            st['p_dsa_ik'].append(ik[None])
            o_a, hist = gated_conv(zs, d_a, conv_a_w[i], _tmajor(state_conv_a[i]), tm=ts, shift=nb)
            st['s_conv_a'].append(_bmajor(hist, nb))
            _, _, _, q, k, v, qi, ik, wi = _col_splits(_bmajor(zs, nb), split_e)
            o_b = dsa_sample(q, qi, wi, k, v, ik, cache_dsa_k[i], cache_dsa_v[i], cache_dsa_ik[i],
                             page_table)
            ys = mm_norm_res([o_a, _tmajor(o_b)], w_out_e, i, g, l, 1, ys, tm=ts, tn=512)
            st['s_dsa_k'].append(k.reshape(nb, nq, -1, HD))
            st['s_dsa_v'].append(v.reshape(nb, nq, -1, HD))
            st['s_dsa_ik'].append(ik)
        else:
            zp = norm_mm(yp, g, l, 0, w_in_o, i, tm=tm_big, tn=512)
            zs = norm_mm(ys, g, l, 0, w_in_o, i, tm=ts, tn=512)
            phi_k = (nsa_pe_k[i], nsa_w1_k[i], nsa_w2_k[i])
            phi_v = (nsa_pe_v[i], nsa_w1_v[i], nsa_w2_v[i])
            kw_d = conv_d_w.shape[1]
            kc, vc, ks, vs, kw, vw = lax.optimization_barrier(
                tuple(zp[:, off_o[j]:off_o[j] + kv_c] for j in range(1, 7)))
            rows2 = lambda a: a.reshape(1, seq * n_kv_c, HD)
            by_group = lambda c: c.reshape(-1, n_kv_c, HD).transpose(1, 0, 2).astype(BF16)
            ck = by_group(nsa_compress(rows2(kc), *phi_k, n_kv_c))
            cv = by_group(nsa_compress(rows2(vc), *phi_v, n_kv_c))
            o_c = nsa_prompt(zp, off_o[0], off_o[9], dq_c // HD, ck, cv, ks.astype(BF16),
                             vs.astype(BF16), kw.astype(BF16), vw.astype(BF16))
            o_d, hist = conformer_conv(zp, off_o[7], off_o[8], conv_d_w[i], conv_d_b[i], ln_d_g[i],
                                       ln_d_b[i], jnp.zeros((_halo_rows(kw_d, 1), d_d), F32),
                                       tm=tm_mid, shift=1)
            st['p_conv_d'].append(hist[hist.shape[0] - (kw_d - 1):][None])
            yp = mm_norm_res([o_c, o_d], w_out_o, i, g, l, 1, yp, tm=tm_mid, tn=512)
            kv4 = lambda a: a.reshape(1, -1, n_kv_c, HD)
            for n, a in zip(('p_cmp_k', 'p_cmp_v', 'p_sel_k', 'p_sel_v'), (kc, vc, ks, vs)):
                st[n].append(kv4(a))
            st['p_win_k'].append(kv4(kw[seq - wb:]))
            st['p_win_v'].append(kv4(vw[seq - wb:]))
            q, kc, vc, ks, vs, kw, vw, _, _, gl = _col_splits(_bmajor(zs, nb), split_o)
            ck = nsa_compress_paged(cache_nsa_cmp_k[i], page_table, *phi_k)
            cv = nsa_compress_paged(cache_nsa_cmp_v[i], page_table, *phi_v)
            win_k, win_v = cache_nsa_win_k[i], cache_nsa_win_v[i]
            o_c = nsa_sample(q, gl, ck, cv, win_k, win_v, ks, vs, kw, vw,
                             cache_nsa_sel_k[i], cache_nsa_sel_v[i], page_table)
            o_d, hist = conformer_conv(zs, off_o[7], off_o[8], conv_d_w[i], conv_d_b[i], ln_d_g[i],
                                       ln_d_b[i], _tmajor(state_conv_d[i]), tm=ts, shift=nb)
            st['s_conv_d'].append(_bmajor(hist, nb))
            ys = mm_norm_res([_tmajor(o_c), o_d], w_out_o, i, g, l, 1, ys, tm=ts, tn=512)
            kv4 = lambda a: a.reshape(nb, nq, n_kv_c, HD)
            for n, a in zip(('s_cmp_k', 's_cmp_v', 's_sel_k', 's_sel_v'), (kc, vc, ks, vs)):
                st[n].append(kv4(a))
            st['s_win_k'].append(jnp.concatenate([win_k, kv4(kw)], axis=1)[:, nq:])
            st['s_win_v'].append(jnp.concatenate([win_v, kv4(vw)], axis=1)[:, nq:])

        n_mem = mem.shape[0]
        mk = norm_mm(mem, mem_g, l, 0, w_mk, l, tm=n_mem, tn=512)
        mv = norm_mm(mem, mem_g, l, 0, w_mv, l, tm=n_mem, tn=512)
        st['p_mem_k'].append(mk.reshape(1, n_mem, -1, HD))
        st['p_mem_v'].append(mv.reshape(1, n_mem, -1, HD))
        yp = cross_attn(yp, g, l, 2, 3, w_mq, w_mo, l, mk, mv, tm=tm_mid)
        qs = norm_mm(ys, g, l, 2, w_mq, l, tm=ts, tn=512)
        a = mem_attn_batched(_bmajor(qs, nb), cache_mem_k, cache_mem_v, l)
        ys = mm_norm_res([_tmajor(a)], w_mo, l, g, l, 3, ys, tm=ts, tn=512)

        kw_f = ffn_conv_w.shape[1]
        gact, hist = ffn_up(yp, g, l, 4, w_up, l, ffn_conv_w[l],
                            jnp.zeros((_halo_rows(kw_f, 1), d_ff), F32), tm=tm_big, shift=1)
        st['p_ffn'].append(hist[hist.shape[0] - (kw_f - 1):][None])
        yp = mm_norm_res([gact], w_down, l, g, l, 5, yp, tm=tm_mid, tn=256)
        gact, hist = ffn_up(ys, g, l, 4, w_up, l, ffn_conv_w[l], _tmajor(state_ffn_conv[l]),
                            tm=ts, shift=nb)
        st['s_ffn'].append(_bmajor(hist, nb))
        ys = mm_norm_res([gact], w_down, l, g, l, 5, ys, tm=ts, tn=256)

    out = {n: jnp.stack(a) for n, a in st.items()}
    return (yp.reshape(1, seq, d), _bmajor(ys, nb)) + tuple(out[n] for n in names)
```

```python
import functools

import jax
import jax.numpy as jnp
from jax import lax
from jax.experimental import pallas as pl
from jax.experimental.pallas import tpu as pltpu

F32 = jnp.float32
BF16 = jnp.bfloat16
I32 = jnp.int32

EPS = 1e-6
NEG = -1e30
BIG = 1e4
HD = 128
PAGE = 128
CMP_BLK = 64
N_SEL = 16
WINDOW = 512
TOPK_MAX = 256
H_IDX = 16
D_IDX = 64
INT_MIN = -(2 ** 31)

VMEM_LIMIT_BYTES = 56 * 1024 * 1024


def _params(*sem):
    return pltpu.CompilerParams(dimension_semantics=sem, vmem_limit_bytes=VMEM_LIMIT_BYTES)


def _rms(x, g):
    return x * lax.rsqrt(jnp.mean(x * x, axis=-1, keepdims=True) + EPS) * g


def _dot(a, b):
    return jnp.dot(a.astype(BF16), b.astype(BF16), preferred_element_type=F32)


def _dot_nt(a, b):
    return lax.dot_general(a.astype(BF16), b.astype(BF16), (((1,), (1,)), ((), ())),
                           preferred_element_type=F32)


def _gain_spec(g, layer):
    return pl.BlockSpec((None,) + g.shape[1:], lambda *_: (layer, 0, 0))


def _norm_mm_kernel(x_ref, g_ref, w_ref, o_ref, xn_ref, *, gr):
    @pl.when(pl.program_id(1) == 0)
    def _():
        xn_ref[...] = _rms(x_ref[...], g_ref[gr:gr + 1, :]).astype(BF16)

    o_ref[...] = jnp.dot(xn_ref[...], w_ref[...].astype(BF16), preferred_element_type=F32)


def norm_mm(x, g, gl, gr, w, wl, *, tm, tn):
    m, k = x.shape
    n = w.shape[2]
    return pl.pallas_call(
        functools.partial(_norm_mm_kernel, gr=gr),
        grid=(m // tm, pl.cdiv(n, tn)),
        in_specs=[pl.BlockSpec((tm, k), lambda i, j: (i, 0)),
                  _gain_spec(g, gl),
                  pl.BlockSpec((None, k, tn), lambda i, j: (wl, 0, j))],
        out_specs=pl.BlockSpec((tm, tn), lambda i, j: (i, j)),
        out_shape=jax.ShapeDtypeStruct((m, n), F32),
        scratch_shapes=[pltpu.VMEM((tm, k), BF16)],
        compiler_params=_params("arbitrary", "arbitrary"),
        name="norm_mm",
    )(x, g, w)


def _mm_norm_res_kernel(*refs, n_in, widths, tn, nj, gr):
    a_refs = refs[:n_in]
    w_ref, g_ref, y_ref, o_ref, a_bf = refs[n_in:]
    j = pl.program_id(1)

    @pl.when(j == 0)
    def _():
        off = 0
        for a_ref, wd in zip(a_refs, widths):
            a_bf[:, off:off + wd] = a_ref[...].astype(BF16)
            off += wd

    col = pl.multiple_of(j * tn, tn)
    o_ref[:, pl.ds(col, tn)] = jnp.dot(a_bf[...], w_ref[...].astype(BF16),
                                       preferred_element_type=F32)

    @pl.when(j == nj - 1)
    def _():
        o_ref[...] = y_ref[...] + _rms(o_ref[...], g_ref[gr:gr + 1, :])


def mm_norm_res(a_list, w, wl, g, gl, gr, y, *, tm, tn, widths=None):
    m, n = y.shape
    widths = tuple(a.shape[1] for a in a_list) if widths is None else tuple(widths)
    k = sum(widths)
    assert w.shape[1:] == (k, n) and n % tn == 0
    nj = n // tn
    kern = functools.partial(_mm_norm_res_kernel, n_in=len(a_list), widths=widths, tn=tn, nj=nj,
                             gr=gr)
    return pl.pallas_call(
        kern,
        grid=(m // tm, nj),
        in_specs=[pl.BlockSpec((tm, wd), lambda i, j: (i, 0)) for wd in widths]
        + [pl.BlockSpec((None, k, tn), lambda i, j: (wl, 0, j)),
           _gain_spec(g, gl),
           pl.BlockSpec((tm, n), lambda i, j: (i, 0))],
        out_specs=pl.BlockSpec((tm, n), lambda i, j: (i, 0)),
        out_shape=jax.ShapeDtypeStruct((m, n), F32),
        scratch_shapes=[pltpu.VMEM((tm, k), BF16)],
        compiler_params=_params("arbitrary", "arbitrary"),
        name="mm_norm_res",
    )(*a_list, w, g, y)


def _order_key(x):
    b = pltpu.bitcast(x, I32)
    return b ^ ((b >> 31) & 0x7FFFFFFF)


def _kth_largest_key(count_ge, k, shape):
    t0 = jnp.where(count_ge(jnp.zeros(shape, I32)) >= k, 0, INT_MIN).astype(I32)

    def bit_body(n, t):
        cand = t | jnp.left_shift(jnp.int32(1), 30 - n)
        return jnp.where(count_ge(cand) >= k, cand, t)

    return lax.fori_loop(0, 31, bit_body, t0)


LOG2E = 1.4426950408889634
M_INIT = 0.1 * NEG


def _stage_queries(q_ref, qg_sc, n_groups, rep, tq):
    c = HD ** -0.5 * LOG2E
    for g in range(n_groups):
        for r in range(rep):
            h = g * rep + r
            qg_sc[g, r * tq:(r + 1) * tq, :] = (q_ref[:, h * HD:(h + 1) * HD] * c).astype(BF16)


def _flash_step(qg, kc, vc, bias, carry):
    m, l, acc = carry
    s = _dot_nt(qg, kc) + bias
    m_new = jnp.maximum(m, jnp.max(s, axis=1, keepdims=True))
    p = jnp.exp2(s - m_new)
    alpha = jnp.exp2(m - m_new)
    return (m_new, alpha * l + jnp.sum(p, axis=1, keepdims=True),
            alpha * acc + jnp.dot(p.astype(BF16), vc, preferred_element_type=F32))


def _flash_init(rows):
    return (jnp.full((rows, 1), M_INIT, F32), jnp.zeros((rows, 1), F32), jnp.zeros((rows, HD), F32))


def _flash_finish(carry):
    _, l, acc = carry
    return acc / jnp.where(l > 0, l, 1.0)


def _dsa_prompt_kernel(q_ref, qia_ref, qib_ref, iw_ref, k_ref, v_ref, ik_ref, o_ref,
                       key_sc, qs_sc, wb_sc, qg_sc, *, tq, tks, tka, topk, n_heads, n_groups):
    i = pl.program_id(0)
    q0 = i * tq
    nch_a = (q0 + tq + tka - 1) // tka
    nch_s = nch_a * (tka // tks)
    rep = n_heads // n_groups
    qpos = q0 + lax.broadcasted_iota(I32, (tq, 1), 0)

    half = H_IDX // 2
    wscale = H_IDX ** -0.5 * D_IDX ** -0.5
    for h in range(H_IDX):
        src = qia_ref if h < half else qib_ref
        hh = h % half
        qs_sc[h * tq:(h + 1) * tq, :] = src[:, hh * D_IDX:(hh + 1) * D_IDX].astype(BF16)
        wb_sc[h] = jnp.broadcast_to(iw_ref[:, D_IDX + h:D_IDX + h + 1] * wscale, (tq, 128))
    _stage_queries(q_ref, qg_sc, n_groups, rep, tq)

    def score_body(c, carry):
        col = pl.multiple_of(c * tks, tks)
        ikc = ik_ref[pl.ds(col, tks), :]
        acc = jnp.zeros((tq, tks), F32)
        for h in range(H_IDX):
            s = lax.dot_general(qs_sc[h * tq:(h + 1) * tq, :], ikc, (((1,), (1,)), ((), ())),
                                preferred_element_type=F32)
            acc = acc + jnp.concatenate([wb_sc[h]] * (tks // 128), axis=1) * jnp.maximum(s, 0.0)
        kpos = col + lax.broadcasted_iota(I32, (1, tks), 1)
        key_sc[:, pl.ds(col, tks)] = jnp.where(kpos <= qpos, _order_key(acc), INT_MIN)
        return carry

    lax.fori_loop(0, nch_s, score_body, 0)

    def count_ge(t):
        tb = jnp.broadcast_to(t, (tq, 128))

        def body(c, cnt):
            col = pl.multiple_of(c * tka, tka)
            for j in range(tka // 128):
                cnt = cnt + jnp.where(key_sc[:, pl.ds(col + j * 128, 128)] >= tb, 1, 0)
            return cnt

        cnt = lax.fori_loop(0, nch_a, body, jnp.zeros((tq, 128), I32))
        return jnp.sum(cnt, axis=1, keepdims=True)

    thr = jnp.maximum(_kth_largest_key(count_ge, topk, (tq, 1)), INT_MIN + 1)

    def bias_body(c, carry):
        col = pl.multiple_of(c * tks, tks)
        bias = jnp.where(key_sc[:, pl.ds(col, tks)] >= thr, 0.0, NEG)
        key_sc[:, pl.ds(col, tks)] = pltpu.bitcast(bias, I32)
        return carry

    lax.fori_loop(0, nch_s, bias_body, 0)

    def att_body(c, carry):
        col = pl.multiple_of(c * tka, tka)
        bias = pltpu.bitcast(key_sc[:, pl.ds(col, tka)], F32)
        bias = jnp.concatenate([bias] * rep, axis=0)
        out = []
        for g in range(n_groups):
            kc = k_ref[pl.ds(col, tka), g * HD:(g + 1) * HD]
            vc = v_ref[pl.ds(col, tka), g * HD:(g + 1) * HD]
            out.append(_flash_step(qg_sc[g], kc, vc, bias, carry[g]))
        return tuple(out)

    res = lax.fori_loop(0, nch_a, att_body, tuple(_flash_init(rep * tq) for _ in range(n_groups)))
    for g in range(n_groups):
        o = _flash_finish(res[g])
        for r in range(rep):
            o_ref[:, (g * rep + r) * HD:(g * rep + r + 1) * HD] = o[r * tq:(r + 1) * tq, :]


def dsa_prompt(z, q_off, qi_off, iw_off, n_heads, k_bf, v_bf, ik_bf, *, tq=128, tks=256, tka=512):
    t = z.shape[0]
    dq = n_heads * HD
    half = H_IDX * D_IDX // 2
    n_groups = k_bf.shape[1] // HD
    assert q_off % dq == 0 and qi_off % half == 0 and iw_off % 128 == 0 and D_IDX + H_IDX <= 128
    assert t % tka == 0 and tka % tks == 0
    topk = min(TOPK_MAX, t // 4)
    kern = functools.partial(_dsa_prompt_kernel, tq=tq, tks=tks, tka=tka, topk=topk,
                             n_heads=n_heads, n_groups=n_groups)
    full = lambda a: pl.BlockSpec(a.shape, lambda i: (0, 0))
    col = lambda w, j: pl.BlockSpec((tq, w), lambda i: (i, j))
    return pl.pallas_call(
        kern,
        grid=(t // tq,),
        in_specs=[col(dq, q_off // dq), col(half, qi_off // half), col(half, qi_off // half + 1),
                  col(128, iw_off // 128), full(k_bf), full(v_bf), full(ik_bf)],
        out_specs=pl.BlockSpec((tq, dq), lambda i: (i, 0)),
        out_shape=jax.ShapeDtypeStruct((t, dq), F32),
        scratch_shapes=[pltpu.VMEM((tq, t), I32), pltpu.VMEM((H_IDX * tq, D_IDX), BF16),
                        pltpu.VMEM((H_IDX, tq, 128), F32),
                        pltpu.VMEM((n_groups, n_heads // n_groups * tq, HD), BF16)],
        compiler_params=_params("arbitrary"),
        name="dsa_prompt",
    )(z, z, z, z, k_bf, v_bf, ik_bf)


def _nsa_compress_kernel(x_ref, pe_ref, w1_ref, w2_ref, o_ref, *, nblk, n_groups):
    _compress_rows(x_ref, pe_ref, w1_ref, w2_ref, o_ref, nblk=nblk, n_groups=n_groups)


def _compress_rows(x_ref, pe_ref, w1_ref, w2_ref, o_ref, *, nblk, n_groups):
    acc = jnp.zeros((n_groups * nblk, HD), F32)
    for p in range(CMP_BLK):
        pe_p = pe_ref[p:p + 1, :]
        lhs = jnp.concatenate(
            [x_ref[pl.ds(p * n_groups + g, nblk, stride=CMP_BLK * n_groups), :] + pe_p
             for g in range(n_groups)], axis=0)
        acc = acc + _dot(lhs, w1_ref[p * HD:(p + 1) * HD, :])
    out = _dot(jnp.maximum(acc, 0.0), w2_ref[...])
    for g in range(n_groups):
        o_ref[:, g * HD:(g + 1) * HD] = out[g * nblk:(g + 1) * nblk, :]


def nsa_compress(x, pe, w1, w2, n_groups):
    b, tg, _ = x.shape
    nblk = tg // (n_groups * CMP_BLK)
    kern = functools.partial(_nsa_compress_kernel, nblk=nblk, n_groups=n_groups)
    full = lambda a: pl.BlockSpec(a.shape, lambda i: (0,) * a.ndim)
    return pl.pallas_call(
        kern,
        grid=(b,),
        in_specs=[pl.BlockSpec((None, tg, HD), lambda i: (i, 0, 0)), full(pe), full(w1), full(w2)],
        out_specs=pl.BlockSpec((None, nblk, n_groups * HD), lambda i: (i, 0, 0)),
        out_shape=jax.ShapeDtypeStruct((b, nblk, n_groups * HD), F32),
        compiler_params=_params("arbitrary"),
        name="nsa_compress",
    )(x, pe, w1, w2)


def _masked_softmax(s, ok):
    s = jnp.where(ok, s, NEG)
    e = jnp.where(ok, jnp.exp(s - jnp.max(s, axis=-1, keepdims=True)), 0.0)
    d = jnp.sum(e, axis=-1, keepdims=True)
    return e / jnp.where(d > 0, d, 1.0)


def _top_blocks(score, n_sel, axis=1):
    nb = score.shape[axis]
    pos = lax.broadcasted_iota(I32, score.shape, axis)
    sel = jnp.zeros(score.shape, F32)
    for _ in range(n_sel):
        m = jnp.max(score, axis=axis, keepdims=True)
        idx = jnp.min(jnp.where(score == m, pos, nb), axis=axis, keepdims=True)
        hit = pos == idx
        sel = jnp.where(hit, 1.0, sel)
        score = jnp.where(hit, -3e38, score)
    return sel


def _masked_softmax2(s, ok):
    s = jnp.where(ok, s, NEG)
    e = jnp.where(ok, jnp.exp2(s - jnp.max(s, axis=-1, keepdims=True)), 0.0)
    d = jnp.sum(e, axis=-1, keepdims=True)
    return e / jnp.where(d > 0, d, 1.0)


def _nsa_prompt_kernel(q_ref, gl_ref, ck_ref, cv_ref, ex_ref, ks_ref, vs_ref, kw_ref, vw_ref, o_ref,
                       qg_sc, bias_sc, *, tq, tk, n_heads, n_groups, nbc, gl_lane):
    i = pl.program_id(0)
    q0 = i * tq
    rep = n_heads // n_groups
    qpos = q0 + lax.broadcasted_iota(I32, (tq, 1), 0)
    jb = lax.broadcasted_iota(I32, (1, nbc), 1)
    cmp_ok = jnp.concatenate([jnp.where(((jb + 1) * CMP_BLK - 1) <= qpos, 1, 0)] * rep, axis=0) > 0
    cur = qpos // CMP_BLK
    gates = jax.nn.sigmoid(gl_ref[:, gl_lane:gl_lane + 3 * n_heads])

    _stage_queries(q_ref, qg_sc, n_groups, rep, tq)
    w0 = pl.multiple_of(jnp.maximum(q0 - WINDOW, 0), tq)
    wlen = WINDOW + tq
    diff = qpos - (w0 + lax.broadcasted_iota(I32, (1, wlen), 1))
    w_ok = jnp.where((diff >= 0) & (diff <= WINDOW), 1, 0)
    w_ok = jnp.concatenate([w_ok] * rep, axis=0) > 0
    imps = []
    for g in range(n_groups):
        p_c = _masked_softmax2(_dot_nt(qg_sc[g], ck_ref[g]), cmp_ok)
        o_c = _dot(p_c, cv_ref[g])
        imp = p_c[0:tq]
        for r in range(1, rep):
            imp = imp + p_c[r * tq:(r + 1) * tq]
        imps.append(imp)
        kw = kw_ref[pl.ds(w0, wlen), g * HD:(g + 1) * HD]
        vw = vw_ref[pl.ds(w0, wlen), g * HD:(g + 1) * HD]
        o_w = _dot(_masked_softmax2(_dot_nt(qg_sc[g], kw), w_ok), vw)
        for r in range(rep):
            h = g * rep + r
            rows = slice(r * tq, (r + 1) * tq)
            o_ref[:, h * HD:(h + 1) * HD] = (gates[:, 3 * h:3 * h + 1] * o_c[rows]
                                             + gates[:, 3 * h + 2:3 * h + 3] * o_w[rows])

    imp_t = jnp.concatenate([imp.T for imp in imps], axis=1)
    jb_t = lax.broadcasted_iota(I32, (nbc, 1), 0)
    cur_t = (q0 + lax.broadcasted_iota(I32, (1, tq), 1)) // CMP_BLK
    cur_t = jnp.concatenate([cur_t] * n_groups, axis=1)
    blk_score = jnp.where(jb_t > cur_t, NEG, jnp.where((jb_t == cur_t) | (jb_t == 0), BIG, imp_t))
    sel_t = _top_blocks(blk_score, min(N_SEL, nbc), axis=0)
    selm = jnp.concatenate([sel_t[:, g * tq:(g + 1) * tq].T for g in range(n_groups)],
                           axis=0).astype(BF16)

    nch = (q0 + tq + tk - 1) // tk

    def bias_body(c, carry):
        col = pl.multiple_of(c * tk, tk)
        picked = jnp.dot(selm, ex_ref[:, pl.ds(col, tk)], preferred_element_type=F32)
        kpos = col + lax.broadcasted_iota(I32, (1, tk), 1)
        for g in range(n_groups):
            ok = (picked[g * tq:(g + 1) * tq] > 0.5) & (kpos <= qpos)
            bias_sc[g, :, pl.ds(col, tk)] = jnp.where(ok, 0.0, NEG)
        return carry

    lax.fori_loop(0, nch, bias_body, 0)

    def att_body(c, carry):
        col = pl.multiple_of(c * tk, tk)
        out = []
        for g in range(n_groups):
            bias = jnp.concatenate([bias_sc[g, :, pl.ds(col, tk)]] * rep, axis=0)
            kc = ks_ref[pl.ds(col, tk), g * HD:(g + 1) * HD]
            vc = vs_ref[pl.ds(col, tk), g * HD:(g + 1) * HD]
            out.append(_flash_step(qg_sc[g], kc, vc, bias, carry[g]))
        return tuple(out)

    res = lax.fori_loop(0, nch, att_body, tuple(_flash_init(rep * tq) for _ in range(n_groups)))

    for g in range(n_groups):
        o_s = _flash_finish(res[g])
        for r in range(rep):
            h = g * rep + r
            o_ref[:, h * HD:(h + 1) * HD] += gates[:, 3 * h + 1:3 * h + 2] * o_s[r * tq:(r + 1) * tq]


def nsa_prompt(z, q_off, gl_off, n_heads, ck_bf, cv_bf, ks_bf, vs_bf, kw_bf, vw_bf, *, tq=128, tk=512):
    t = z.shape[0]
    dq = n_heads * HD
    n_groups, nbc, _ = ck_bf.shape
    gl_lane = gl_off % 128
    assert t >= WINDOW + tq and t % tk == 0 and nbc == t // CMP_BLK
    assert q_off % dq == 0 and gl_lane + 3 * n_heads <= 128
    kern = functools.partial(_nsa_prompt_kernel, tq=tq, tk=tk, n_heads=n_heads,
                             n_groups=n_groups, nbc=nbc, gl_lane=gl_lane)
    expand = (jnp.arange(t, dtype=I32)[None, :] // CMP_BLK
              == jnp.arange(nbc, dtype=I32)[:, None]).astype(BF16)
    full = lambda a: pl.BlockSpec(a.shape, lambda i: (0,) * a.ndim)
    col = lambda w, j: pl.BlockSpec((tq, w), lambda i: (i, j))
    return pl.pallas_call(
        kern,
        grid=(t // tq,),
        in_specs=[col(dq, q_off // dq), col(128, gl_off // 128), full(ck_bf), full(cv_bf),
                  full(expand), full(ks_bf), full(vs_bf), full(kw_bf), full(vw_bf)],
        out_specs=pl.BlockSpec((tq, dq), lambda i: (i, 0)),
        out_shape=jax.ShapeDtypeStruct((t, dq), F32),
        scratch_shapes=[pltpu.VMEM((n_groups, n_heads // n_groups * tq, HD), BF16),
                        pltpu.VMEM((n_groups, tq, t), F32)],
        compiler_params=_params("arbitrary"),
        name="nsa_prompt",
    )(z, z, ck_bf, cv_bf, expand, ks_bf, vs_bf, kw_bf, vw_bf)


def _halo_rows(kw, shift):
    return max(8, (kw - 1) * shift)


def _dwconv(ext_ref, w_ref, u, prev_ref, first, *, tm, kw, shift):
    hp = _halo_rows(kw, shift)

    @pl.when(first)
    def _():
        ext_ref[0:hp, :] = prev_ref[...]

    ext_ref[hp:hp + tm, :] = u
    y = None
    for i in range(kw):
        term = w_ref[i:i + 1, :] * ext_ref[pl.ds(hp - (kw - 1 - i) * shift, tm), :]
        y = term if y is None else y + term
    tail = ext_ref[tm:tm + hp, :]
    ext_ref[0:hp, :] = tail
    return y, tail


def _gated_conv_kernel(xa_ref, bg_ref, cg_ref, w_ref, prev_ref, o_ref, st_ref, ext_ref,
                       *, tm, kw, shift):
    u = cg_ref[...] * xa_ref[...]
    y, tail = _dwconv(ext_ref, w_ref, u, prev_ref, pl.program_id(0) == 0, tm=tm, kw=kw, shift=shift)
    o_ref[...] = bg_ref[...] * y
    st_ref[...] = tail


def gated_conv(z, c, w, prev, *, tm, shift):
    m = z.shape[0]
    kw = w.shape[0]
    hp = _halo_rows(kw, shift)
    kern = functools.partial(_gated_conv_kernel, tm=tm, kw=kw, shift=shift)
    col = lambda j: pl.BlockSpec((tm, c), lambda i: (i, j))
    return pl.pallas_call(
        kern,
        grid=(m // tm,),
        in_specs=[col(0), col(1), col(2), pl.BlockSpec((kw, c), lambda i: (0, 0)),
                  pl.BlockSpec((hp, c), lambda i: (0, 0))],
        out_specs=[pl.BlockSpec((tm, c), lambda i: (i, 0)), pl.BlockSpec((hp, c), lambda i: (0, 0))],
        out_shape=[jax.ShapeDtypeStruct((m, c), F32), jax.ShapeDtypeStruct((hp, c), F32)],
        scratch_shapes=[pltpu.VMEM((hp + tm, c), F32)],
        compiler_params=_params("arbitrary"),
        name="gated_conv",
    )(z, z, z, w, prev)


def _conformer_kernel(dpa_ref, dpb_ref, dga_ref, dgb_ref, w_ref, b_ref, lg_ref, lb_ref, prev_ref,
                      o_ref, st_ref, ext_ref, *, tm, kw, shift):
    u = jnp.concatenate([dpa_ref[...] * jax.nn.sigmoid(dga_ref[...]),
                         dpb_ref[...] * jax.nn.sigmoid(dgb_ref[...])], axis=1)
    c, tail = _dwconv(ext_ref, w_ref, u, prev_ref, pl.program_id(0) == 0, tm=tm, kw=kw, shift=shift)
    c = c + b_ref[...]
    mu = jnp.mean(c, axis=-1, keepdims=True)
    xc = c - mu
    y = xc * lax.rsqrt(jnp.mean(xc * xc, axis=-1, keepdims=True) + EPS) * lg_ref[...] + lb_ref[...]
    o_ref[...] = y * jax.nn.sigmoid(y)
    st_ref[...] = tail


def conformer_conv(z, dp_off, dg_off, w, b, ln_g, ln_b, prev, *, tm, shift):
    m = z.shape[0]
    kw, c = w.shape
    half = c // 2
    assert dp_off % half == 0 and dg_off % half == 0
    hp = _halo_rows(kw, shift)
    kern = functools.partial(_conformer_kernel, tm=tm, kw=kw, shift=shift)
    col = lambda off, j: pl.BlockSpec((tm, half), lambda i: (i, off // half + j))
    row = pl.BlockSpec((tm, c), lambda i: (i, 0))
    vec = pl.BlockSpec((1, c), lambda i: (0, 0))
    return pl.pallas_call(
        kern,
        grid=(m // tm,),
        in_specs=[col(dp_off, 0), col(dp_off, 1), col(dg_off, 0), col(dg_off, 1),
                  pl.BlockSpec((kw, c), lambda i: (0, 0)), vec, vec, vec,
                  pl.BlockSpec((hp, c), lambda i: (0, 0))],
        out_specs=[row, pl.BlockSpec((hp, c), lambda i: (0, 0))],
        out_shape=[jax.ShapeDtypeStruct((m, c), F32), jax.ShapeDtypeStruct((hp, c), F32)],
        scratch_shapes=[pltpu.VMEM((hp + tm, c), F32)],
        compiler_params=_params("arbitrary"),
        name="conformer_conv",
    )(z, z, z, z, w, b.reshape(1, c), ln_g.reshape(1, c), ln_b.reshape(1, c), prev)


FFN_TF = 128
FFN_NB = 4


def _ffn_up_kernel(*refs, tm, kw, shift, gr):
    y_ref, g_ref = refs[:2]
    wa_refs = refs[2:2 + FFN_NB]
    wv_refs = refs[2 + FFN_NB:2 + 2 * FFN_NB]
    cw_ref, prev_ref, o_ref, st_ref, xn_ref, wcat_ref, halo_ref, ext_ref = refs[2 + 2 * FFN_NB:]
    i = pl.program_id(0)
    j = pl.program_id(1)
    hp = _halo_rows(kw, shift)
    tf = FFN_NB * FFN_TF

    @pl.when(j == 0)
    def _():
        xn_ref[...] = _rms(y_ref[...], g_ref[gr:gr + 1, :]).astype(BF16)

    for s in range(FFN_NB):
        wcat_ref[:, s * FFN_TF:(s + 1) * FFN_TF] = wa_refs[s][...].astype(BF16)
        wcat_ref[:, tf + s * FFN_TF:tf + (s + 1) * FFN_TF] = wv_refs[s][...].astype(BF16)
    h = jnp.dot(xn_ref[...], wcat_ref[...], preferred_element_type=F32)
    a = h[:, 0:tf]
    v = h[:, tf:2 * tf]

    @pl.when(i > 0)
    def _():
        ext_ref[0:hp, :] = halo_ref[j]

    c, tail = _dwconv(ext_ref, cw_ref, a, prev_ref, i == 0, tm=tm, kw=kw, shift=shift)
    halo_ref[j] = tail
    st_ref[...] = tail
    o_ref[...] = (c * jax.nn.sigmoid(c) * v).astype(BF16)


def ffn_up(y, g, gl, gr, w_up, wl, conv_w, prev, *, tm, shift):
    m, d = y.shape
    kw, f = conv_w.shape
    nf = f // FFN_TF
    assert f % FFN_TF == 0 and w_up.shape[1:] == (d, 2 * f)
    tf = FFN_NB * FFN_TF
    nj = pl.cdiv(nf, FFN_NB)
    fp = nj * tf
    hp = _halo_rows(kw, shift)
    nm = m // tm
    conv_w = jnp.pad(conv_w, ((0, 0), (0, fp - f)))
    prev = jnp.pad(prev, ((0, 0), (0, fp - f)))
    kern = functools.partial(_ffn_up_kernel, tm=tm, kw=kw, shift=shift, gr=gr)

    def granule(base, s):
        return pl.BlockSpec((None, d, FFN_TF),
                            lambda i, j: (wl, 0, base + jnp.minimum(FFN_NB * j + s, nf - 1)))

    return pl.pallas_call(
        kern,
        grid=(nm, nj),
        in_specs=[pl.BlockSpec((tm, d), lambda i, j: (i, 0)), _gain_spec(g, gl)]
        + [granule(0, s) for s in range(FFN_NB)] + [granule(nf, s) for s in range(FFN_NB)]
        + [pl.BlockSpec((kw, tf), lambda i, j: (0, j)), pl.BlockSpec((hp, tf), lambda i, j: (0, j))],
        out_specs=[pl.BlockSpec((tm, tf), lambda i, j: (i, j)),
                   pl.BlockSpec((hp, tf), lambda i, j: (i, j))],
        out_shape=[jax.ShapeDtypeStruct((m, fp), BF16), jax.ShapeDtypeStruct((nm * hp, fp), F32)],
        scratch_shapes=[pltpu.VMEM((tm, d), BF16), pltpu.VMEM((d, 2 * tf), BF16),
                        pltpu.VMEM((nj, hp, tf), F32), pltpu.VMEM((hp + tm, tf), F32)],
        compiler_params=_params("arbitrary", "arbitrary"),
        name="ffn_up",
    )(y, g, *([w_up] * (2 * FFN_NB)), conv_w, prev)


def _mem_heads(q, mk_head, mv_head, n_heads):
    scale = HD ** -0.5
    outs = []
    for h in range(n_heads):
        s = _dot_nt(q[:, h * HD:(h + 1) * HD], mk_head(h)) * scale
        e = jnp.exp(s - jnp.max(s, axis=-1, keepdims=True))
        p = e / jnp.sum(e, axis=-1, keepdims=True)
        outs.append(_dot(p, mv_head(h)))
    return jnp.concatenate(outs, axis=1)


def _xattn_kernel(y_ref, g_ref, wq_ref, mk_ref, mv_ref, wo_ref, o_ref, wq_bf, wo_bf,
                  *, n_heads, gr_in, gr_out):
    @pl.when(pl.program_id(0) == 0)
    def _():
        wq_bf[...] = wq_ref[...].astype(BF16)
        wo_bf[...] = wo_ref[...].astype(BF16)

    y = y_ref[...]
    q = jnp.dot(_rms(y, g_ref[gr_in:gr_in + 1, :]).astype(BF16), wq_bf[...],
                preferred_element_type=F32)
    o = _mem_heads(q, lambda h: mk_ref[:, h * HD:(h + 1) * HD],
                   lambda h: mv_ref[:, h * HD:(h + 1) * HD], n_heads)
    f = jnp.dot(o.astype(BF16), wo_bf[...], preferred_element_type=F32)
    o_ref[...] = y + _rms(f, g_ref[gr_out:gr_out + 1, :])


def cross_attn(y, g, gl, gr_in, gr_out, wq, wo, wl, mk, mv, *, tm):
    m, d = y.shape
    dh = wq.shape[2]
    kern = functools.partial(_xattn_kernel, n_heads=dh // HD, gr_in=gr_in, gr_out=gr_out)
    full = lambda a: pl.BlockSpec(a.shape, lambda i: (0, 0))
    layer = lambda a: pl.BlockSpec((None,) + a.shape[1:], lambda i: (wl, 0, 0))
    row = pl.BlockSpec((tm, d), lambda i: (i, 0))
    return pl.pallas_call(
        kern,
        grid=(m // tm,),
        in_specs=[row, _gain_spec(g, gl), layer(wq), full(mk), full(mv), layer(wo)],
        out_specs=row,
        out_shape=jax.ShapeDtypeStruct((m, d), F32),
        scratch_shapes=[pltpu.VMEM(wq.shape[1:], BF16), pltpu.VMEM(wo.shape[1:], BF16)],
        compiler_params=_params("arbitrary"),
        name="cross_attn",
    )(y, g, wq, mk, mv, wo)


def _mem_attn_kernel(q_ref, mk_ref, mv_ref, o_ref, *, n_heads):
    o_ref[...] = _mem_heads(q_ref[...], lambda h: mk_ref[:, h, :], lambda h: mv_ref[:, h, :], n_heads)


def mem_attn_batched(q, mk, mv, layer):
    b, t, dh = q.shape
    kern = functools.partial(_mem_attn_kernel, n_heads=dh // HD)
    cache = pl.BlockSpec((None, None) + mk.shape[2:], lambda i: (layer, i, 0, 0, 0))
    blk = pl.BlockSpec((None, t, dh), lambda i: (i, 0, 0))
    return pl.pallas_call(
        kern,
        grid=(b,),
        in_specs=[blk, cache, cache],
        out_specs=blk,
        out_shape=jax.ShapeDtypeStruct(q.shape, F32),
        compiler_params=_params("arbitrary"),
        name="mem_attn_batched",
    )(q, mk, mv)


def _page_copy(cache, pt_ref, b, p, buf, slot, sem):
    rows = cache.shape[1]
    return pltpu.make_async_copy(cache.at[pt_ref[b, p]],
                                 buf.at[slot, pl.ds(p * rows, rows), :], sem.at[slot])


def _for_pages(caches, bufs, sems, pt_ref, b, slot, n_pages, fn):
    def body(p, carry):
        for cache, buf, sem in zip(caches, bufs, sems):
            fn(_page_copy(cache, pt_ref, b, p, buf, slot, sem))
        return carry

    lax.fori_loop(0, n_pages, body, 0)


def _fetch_pages(caches, bufs, sems, pt_ref, n_pages):
    b = pl.program_id(0)
    slot = b % 2
    args = (caches, bufs, sems, pt_ref)

    @pl.when(b == 0)
    def _():
        _for_pages(*args, b, slot, n_pages, lambda cp: cp.start())

    @pl.when(b + 1 < pl.num_programs(0))
    def _():
        _for_pages(*args, b + 1, 1 - slot, n_pages, lambda cp: cp.start())

    _for_pages(*args, b, slot, n_pages, lambda cp: cp.wait())
    return slot


def _paged_call(kern, pt, inputs, n_any, out_shape, out_block, scratch, name):
    nb = pt.shape[0]

    def spec(a):
        nd = a.ndim - 1
        return pl.BlockSpec((None,) + a.shape[1:], lambda i, pt_ref: (i,) + (0,) * nd)

    def shared(a):
        nd = a.ndim
        return pl.BlockSpec(a.shape, lambda i, pt_ref: (0,) * nd)

    in_specs = []
    for a in inputs[:len(inputs) - n_any]:
        in_specs.append(spec(a) if a.shape[0] == nb and a.ndim >= 3 else shared(a))
    in_specs += [pl.BlockSpec(memory_space=pl.ANY)] * n_any
    nd_o = len(out_block)
    return pl.pallas_call(
        kern,
        grid_spec=pltpu.PrefetchScalarGridSpec(
            num_scalar_prefetch=1,
            grid=(nb,),
            in_specs=in_specs,
            out_specs=pl.BlockSpec((None,) + tuple(out_block), lambda i, pt_ref: (i,) + (0,) * nd_o),
            scratch_shapes=scratch),
        out_shape=jax.ShapeDtypeStruct((nb,) + tuple(out_block), out_shape),
        compiler_params=_params("arbitrary"),
        name=name,
    )(pt, *inputs)


def _attend_two_parts(qg, k_a, v_a, ok_a, k_b, v_b, ok_b):
    scale = HD ** -0.5
    s_a = jnp.where(ok_a, _dot_nt(qg, k_a) * scale, NEG)
    s_b = jnp.where(ok_b, _dot_nt(qg, k_b) * scale, NEG)
    mx = jnp.maximum(jnp.max(s_a, axis=1, keepdims=True), jnp.max(s_b, axis=1, keepdims=True))
    e_a = jnp.where(ok_a, jnp.exp(s_a - mx), 0.0)
    e_b = jnp.where(ok_b, jnp.exp(s_b - mx), 0.0)
    den = jnp.sum(e_a, axis=1, keepdims=True) + jnp.sum(e_b, axis=1, keepdims=True)
    return (_dot(e_a, v_a) + _dot(e_b, v_b)) / jnp.where(den > 0, den, 1.0)


def _dsa_sample_select_kernel(pt_ref, qi_ref, wi_ref, ikn_ref, ik_hbm, m_ref, ikbuf, key_sc, sem,
                              *, n_pages, tk, topk, n_new, nq, rows):
    slot = _fetch_pages([ik_hbm], [ikbuf], [sem], pt_ref, n_pages)
    past = n_pages * PAGE
    qi = qi_ref[...].astype(BF16)
    w = wi_ref[...] * (H_IDX ** -0.5 * D_IDX ** -0.5)

    def scores(ikc):
        r = jnp.maximum(_dot_nt(qi, ikc), 0.0) * w
        return jnp.sum(r.reshape(H_IDX, rows, ikc.shape[0]), axis=0)

    def score_body(c, carry):
        col = pl.multiple_of(c * tk, tk)
        key_sc[:, pl.ds(col, tk)] = _order_key(scores(ikbuf[slot, pl.ds(col, tk), :]))
        return carry

    lax.fori_loop(0, past // tk, score_body, 0)
    t_row = lax.rem(lax.broadcasted_iota(I32, (rows, PAGE), 0), nq)
    col = lax.broadcasted_iota(I32, (rows, PAGE), 1)
    new_ok = (col < n_new) & (col <= t_row)
    key_sc[:, past:past + PAGE] = jnp.where(new_ok, _order_key(scores(ikn_ref[...])), INT_MIN)

    def count_ge(t):
        return jnp.sum(jnp.where(key_sc[...] >= t, 1, 0), axis=1, keepdims=True)

    thr = jnp.maximum(_kth_largest_key(count_ge, topk, (rows, 1)), INT_MIN + 1)
    m_ref[...] = jnp.where(key_sc[...] >= thr, 1.0, 0.0)


def _dsa_sample_attn_kernel(pt_ref, q_ref, m_ref, kn_ref, vn_ref, k_hbm, v_hbm, o_ref,
                            kbuf, vbuf, ksem, vsem, *, n_pages, n_groups):
    slot = _fetch_pages([k_hbm, v_hbm], [kbuf, vbuf], [ksem, vsem], pt_ref, n_pages)
    past = n_pages * PAGE
    tile = q_ref.shape[1] // m_ref.shape[0]
    ok_c = jnp.concatenate([m_ref[:, 0:past]] * tile, axis=0) > 0.5
    ok_n = jnp.concatenate([m_ref[:, past:past + PAGE]] * tile, axis=0) > 0.5
    for g in range(n_groups):
        o_ref[g] = _attend_two_parts(
            q_ref[g].astype(BF16),
            kbuf[slot, pl.ds(g, past, stride=n_groups), :],
            vbuf[slot, pl.ds(g, past, stride=n_groups), :], ok_c,
            kn_ref[pl.ds(g, PAGE, stride=n_groups), :],
            vn_ref[pl.ds(g, PAGE, stride=n_groups), :], ok_n)


def _group_rows(x, n_groups):
    b, t, dq = x.shape
    rep = dq // HD // n_groups
    return x.reshape(b, t, n_groups, rep, HD).transpose(0, 2, 3, 1, 4).reshape(b, n_groups, rep * t, HD)


def _ungroup_rows(o, t):
    b, g, rows, _ = o.shape
    rep = rows // t
    return o.reshape(b, g, rep, t, HD).transpose(0, 3, 1, 2, 4).reshape(b, t, g * rep * HD)


def _pad_new_rows(x, n_groups):
    b, t, _ = x.shape
    x = x.reshape(b, t * n_groups, HD)
    return jnp.pad(x, ((0, 0), (0, (PAGE - t) * n_groups), (0, 0)))


def dsa_sample(q, qi, wi, k_new, v_new, ik_new, cache_k, cache_v, cache_ik, pt, *, tk=1024):
    nb, nq, dq = q.shape
    n_pages = pt.shape[1]
    n_groups = cache_k.shape[2]
    rep = dq // HD // n_groups
    rows = rep * nq
    past = n_pages * PAGE
    topk = min(TOPK_MAX, (past + nq) // 4)
    width = past + PAGE
    rep_s = max(1, 8 // nq)
    rows_s = rep_s * nq
    assert rows % rows_s == 0
    qi_r = jnp.broadcast_to(qi.reshape(nb, nq, H_IDX, D_IDX).transpose(0, 2, 1, 3)[:, :, None],
                            (nb, H_IDX, rep_s, nq, D_IDX)).reshape(nb, H_IDX * rows_s, D_IDX)
    wi_r = jnp.broadcast_to(wi.transpose(0, 2, 1)[:, :, None],
                            (nb, H_IDX, rep_s, nq)).reshape(nb, H_IDX * rows_s, 1)
    ikn = jnp.pad(ik_new, ((0, 0), (0, PAGE - nq), (0, 0)))
    sel_kern = functools.partial(_dsa_sample_select_kernel, n_pages=n_pages, tk=tk, topk=topk,
                                 n_new=nq, nq=nq, rows=rows_s)
    mask = _paged_call(
        sel_kern, pt, [qi_r, wi_r, ikn, cache_ik], 1, F32, (rows_s, width),
        [pltpu.VMEM((2, past, D_IDX), F32), pltpu.VMEM((rows_s, width), I32),
         pltpu.SemaphoreType.DMA((2,))], "dsa_sample_select")
    ck = cache_k.reshape(cache_k.shape[0], PAGE * n_groups, HD)
    cv = cache_v.reshape(cache_v.shape[0], PAGE * n_groups, HD)
    att_kern = functools.partial(_dsa_sample_attn_kernel, n_pages=n_pages, n_groups=n_groups)
    o = _paged_call(
        att_kern, pt, [_group_rows(q, n_groups), mask, _pad_new_rows(k_new, n_groups),
                       _pad_new_rows(v_new, n_groups), ck, cv], 2, F32, (n_groups, rows, HD),
        [pltpu.VMEM((2, past * n_groups, HD), F32), pltpu.VMEM((2, past * n_groups, HD), F32),
         pltpu.SemaphoreType.DMA((2,)), pltpu.SemaphoreType.DMA((2,))], "dsa_sample_attn")
    return _ungroup_rows(o, nq)


def _nsa_compress_paged_kernel(pt_ref, pe_ref, w1_ref, w2_ref, x_hbm, o_ref, buf, sem,
                               *, n_pages, n_groups):
    slot = _fetch_pages([x_hbm], [buf], [sem], pt_ref, n_pages)
    _compress_rows(buf.at[slot], pe_ref, w1_ref, w2_ref, o_ref,
                   nblk=n_pages * PAGE // CMP_BLK, n_groups=n_groups)


def nsa_compress_paged(cache, pt, pe, w1, w2):
    n_groups = cache.shape[2]
    n_pages = pt.shape[1]
    past = n_pages * PAGE
    kern = functools.partial(_nsa_compress_paged_kernel, n_pages=n_pages, n_groups=n_groups)
    view = cache.reshape(cache.shape[0], PAGE * n_groups, HD)
    return _paged_call(
        kern, pt, [pe, w1.astype(BF16), w2, view], 1, F32, (past // CMP_BLK, n_groups * HD),
        [pltpu.VMEM((2, past * n_groups, HD), F32), pltpu.SemaphoreType.DMA((2,))],
        "nsa_compress_paged")


def _nsa_sample_kernel(pt_ref, q_ref, gl_ref, ck_ref, cv_ref, ex_ref, wk_ref, wv_ref, skn_ref, svn_ref,
                       wkn_ref, wvn_ref, ks_hbm, vs_hbm, o_ref, kbuf, vbuf, ksem, vsem,
                       *, n_pages, n_groups, nq, n_new):
    slot = _fetch_pages([ks_hbm, vs_hbm], [kbuf, vbuf], [ksem, vsem], pt_ref, n_pages)
    past = n_pages * PAGE
    scale = HD ** -0.5
    rows = q_ref.shape[1]
    rep = rows // nq
    nbc = ck_ref.shape[0]
    wlen = wk_ref.shape[0] // n_groups
    t_row = lax.rem(lax.broadcasted_iota(I32, (rows, 1), 0), nq)
    qpos = past + t_row
    jb = lax.broadcasted_iota(I32, (1, nbc), 1)
    cmp_ok = ((jb + 1) * CMP_BLK - 1) <= qpos
    cur = qpos // CMP_BLK
    newcol = lax.broadcasted_iota(I32, (1, PAGE), 1)
    new_ok = (newcol < n_new) & (newcol <= t_row)
    diff = qpos - (past - wlen + lax.broadcasted_iota(I32, (1, wlen), 1))
    win_ok = (diff >= 0) & (diff <= WINDOW)
    o_cs, scores = [], []
    for g in range(n_groups):
        cols = slice(g * HD, (g + 1) * HD)
        p_c = _masked_softmax(_dot_nt(q_ref[g].astype(BF16), ck_ref[:, cols]) * scale, cmp_ok)
        o_cs.append(_dot(p_c, cv_ref[:, cols]))
        imp = p_c
        for r in range(1, rep):
            imp = imp + pltpu.roll(p_c, r * nq, 0)
        scores.append(jnp.where(jb > cur, NEG, jnp.where((jb == cur) | (jb == 0), BIG, imp)))
    selm = _top_blocks(jnp.concatenate(scores, axis=0).T, N_SEL - 1, axis=0).T.astype(BF16)
    picked_all = jnp.dot(selm, ex_ref[...], preferred_element_type=F32)
    for g in range(n_groups):
        qg = q_ref[g].astype(BF16)
        gates = jax.nn.sigmoid(gl_ref[g])
        o_c = o_cs[g]
        picked = picked_all[g * rows:(g + 1) * rows] > 0.5
        o_s = _attend_two_parts(
            qg, kbuf[slot, pl.ds(g, past, stride=n_groups), :],
            vbuf[slot, pl.ds(g, past, stride=n_groups), :], picked,
            skn_ref[pl.ds(g, PAGE, stride=n_groups), :],
            svn_ref[pl.ds(g, PAGE, stride=n_groups), :], new_ok)
        o_w = _attend_two_parts(
            qg, wk_ref[pl.ds(g, wlen, stride=n_groups), :],
            wv_ref[pl.ds(g, wlen, stride=n_groups), :], win_ok,
            wkn_ref[pl.ds(g, PAGE, stride=n_groups), :],
            wvn_ref[pl.ds(g, PAGE, stride=n_groups), :], new_ok)
        o_ref[g] = gates[:, 0:1] * o_c + gates[:, 1:2] * o_s + gates[:, 2:3] * o_w


def nsa_sample(q, gl, ck, cv, win_k, win_v, ks_new, vs_new, kw_new, vw_new, cache_ks, cache_vs, pt):
    nb, nq, dq = q.shape
    n_pages = pt.shape[1]
    n_groups = cache_ks.shape[2]
    n_heads = dq // HD
    rep = n_heads // n_groups
    rows = rep * nq
    past = n_pages * PAGE
    assert past % CMP_BLK == 0 and nq <= CMP_BLK and ck.shape[1] == past // CMP_BLK
    assert win_k.shape[1] == WINDOW
    glr = gl.reshape(nb, nq, n_groups, rep, 3).transpose(0, 2, 3, 1, 4).reshape(nb, n_groups, rows, 3)
    view = lambda c: c.reshape(c.shape[0], c.shape[1] * n_groups, HD)
    kern = functools.partial(_nsa_sample_kernel, n_pages=n_pages, n_groups=n_groups,
                             nq=nq, n_new=nq)
    pad = lambda x: _pad_new_rows(x, n_groups)
    expand = (jnp.arange(past, dtype=I32)[None, :] // CMP_BLK
              == jnp.arange(past // CMP_BLK, dtype=I32)[:, None]).astype(BF16)
    o = _paged_call(
        kern, pt, [_group_rows(q, n_groups), glr, ck, cv, expand, view(win_k), view(win_v),
                   pad(ks_new), pad(vs_new), pad(kw_new), pad(vw_new), view(cache_ks), view(cache_vs)],
        2, F32, (n_groups, rows, HD),
        [pltpu.VMEM((2, past * n_groups, HD), F32), pltpu.VMEM((2, past * n_groups, HD), F32),
         pltpu.SemaphoreType.DMA((2,)), pltpu.SemaphoreType.DMA((2,))], "nsa_sample")
    return _ungroup_rows(o, nq)


def _tmajor(a):
    b, t, c = a.shape
    return jnp.swapaxes(a, 0, 1).reshape(t * b, c)


def _bmajor(a, b):
    tb, c = a.shape
    return jnp.swapaxes(a.reshape(tb // b, b, c), 0, 1)


def _col_splits(z, sizes):
    out, off = [], 0
    for s in sizes:
        out.append(z[..., off:off + s])
        off += s
    return out


def _row_tile(m, pref):
    return pref if m % pref == 0 else m


def kernel(x_prompt, x_sample, state_conv_a, cache_dsa_k, cache_dsa_v, cache_dsa_ik, cache_nsa_cmp_k, cache_nsa_cmp_v, cache_nsa_sel_k, cache_nsa_sel_v, cache_nsa_win_k, cache_nsa_win_v, state_conv_d, state_ffn_conv, cache_mem_k, cache_mem_v, page_table, mem_prompt, w_in_e, conv_a_w, w_out_e, w_in_o, nsa_pe_k, nsa_w1_k, nsa_w2_k, nsa_pe_v, nsa_w1_v, nsa_w2_v, conv_d_w, conv_d_b, ln_d_g, ln_d_b, w_out_o, norm_g, mem_norm_g, w_mq, w_mk, w_mv, w_mo, w_up, ffn_conv_w, w_down):
    bp, seq, d = x_prompt.shape
    nb, nq, _ = x_sample.shape
    assert bp == 1
    depth = norm_g.shape[0]
    ts = nb * nq
    d_a = conv_a_w.shape[-1]
    d_d = conv_d_w.shape[-1]
    d_ff = ffn_conv_w.shape[-1]
    kv_b = cache_dsa_k.shape[-2] * HD
    kv_c = cache_nsa_cmp_k.shape[-2] * HD
    n_kv_c = cache_nsa_cmp_k.shape[-2]
    dq_b = w_out_e.shape[1] - d_a
    dq_c = w_out_o.shape[1] - d_d
    n_gate = 3 * dq_c // HD
    split_e = (d_a, d_a, d_a, dq_b, kv_b, kv_b, H_IDX * D_IDX, D_IDX, H_IDX)
    split_o = (dq_c,) + (kv_c,) * 6 + (d_d, d_d, n_gate)
    gl_src = dq_c + 6 * kv_c
    w_in_o = jnp.concatenate([w_in_o[:, :, :gl_src], w_in_o[:, :, gl_src + n_gate:],
                              w_in_o[:, :, gl_src:gl_src + n_gate]], axis=2).astype(BF16)
    w_in_e, w_out_e, w_out_o, w_mq, w_mo, w_up, w_down = (
        w.astype(BF16) for w in (w_in_e, w_out_e, w_out_o, w_mq, w_mo, w_up, w_down))
    off_e = [sum(split_e[:j]) for j in range(len(split_e))]
    off_o = [sum(split_o[:j]) for j in range(len(split_o))]
    mem_g = mem_norm_g.reshape(depth, 1, d)
    tm_big = _row_tile(seq, 1024)
    tm_mid = _row_tile(seq, 512)
    wb = min(WINDOW, seq)

    yp = x_prompt.reshape(seq, d)
    ys = _tmajor(x_sample)
    mem = mem_prompt.reshape(mem_prompt.shape[1], d)
    names = ('p_conv_a', 'p_dsa_k', 'p_dsa_v', 'p_dsa_ik', 'p_cmp_k', 'p_cmp_v', 'p_sel_k', 'p_sel_v',
             'p_win_k', 'p_win_v', 'p_conv_d', 'p_ffn', 'p_mem_k', 'p_mem_v',
             's_conv_a', 's_dsa_k', 's_dsa_v', 's_dsa_ik', 's_cmp_k', 's_cmp_v', 's_sel_k', 's_sel_v',
             's_win_k', 's_win_v', 's_conv_d', 's_ffn')
    st = {n: [] for n in names}

    for l in range(depth):
        g = norm_g
        i = l // 2
        if l % 2 == 0:
            zp = norm_mm(yp, g, l, 0, w_in_e, i, tm=tm_big, tn=1024)
            zs = norm_mm(ys, g, l, 0, w_in_e, i, tm=ts, tn=512)
            kw_a = conv_a_w.shape[1]
            o_a, hist = gated_conv(zp, d_a, conv_a_w[i], jnp.zeros((_halo_rows(kw_a, 1), d_a), F32),
                                   tm=tm_mid, shift=1)
            st['p_conv_a'].append(hist[hist.shape[0] - (kw_a - 1):][None])
            k, v, ik = lax.optimization_barrier(
                tuple(zp[:, off_e[j]:off_e[j] + split_e[j]] for j in (4, 5, 7)))
            o_b = dsa_prompt(zp, off_e[3], off_e[6], off_e[7], dq_b // HD,
                             k.astype(BF16), v.astype(BF16), ik.astype(BF16))
            yp = mm_norm_res([o_a, o_b], w_out_e, i, g, l, 1, yp, tm=tm_mid, tn=512)
            st['p_dsa_k'].append(k.reshape(1, seq, -1, HD))
            st['p_dsa_v'].append(v.reshape(1, seq, -1, HD))
            st['p_dsa_ik'].append(ik[None])
            o_a, hist = gated_conv(zs, d_a, conv_a_w[i], _tmajor(state_conv_a[i]), tm=ts, shift=nb)
            st['s_conv_a'].append(_bmajor(hist, nb))
            _, _, _, q, k, v, qi, ik, wi = _col_splits(_bmajor(zs, nb), split_e)
            o_b = dsa_sample(q, qi, wi, k, v, ik, cache_dsa_k[i], cache_dsa_v[i], cache_dsa_ik[i],
                             page_table)
            ys = mm_norm_res([o_a, _tmajor(o_b)], w_out_e, i, g, l, 1, ys, tm=ts, tn=512)
            st['s_dsa_k'].append(k.reshape(nb, nq, -1, HD))
            st['s_dsa_v'].append(v.reshape(nb, nq, -1, HD))
            st['s_dsa_ik'].append(ik)
        else:
            zp = norm_mm(yp, g, l, 0, w_in_o, i, tm=tm_big, tn=1024)
            zs = norm_mm(ys, g, l, 0, w_in_o, i, tm=ts, tn=512)
            phi_k = (nsa_pe_k[i], nsa_w1_k[i], nsa_w2_k[i])
            phi_v = (nsa_pe_v[i], nsa_w1_v[i], nsa_w2_v[i])
            kw_d = conv_d_w.shape[1]
            kc, vc, ks, vs, kw, vw = lax.optimization_barrier(
                tuple(zp[:, off_o[j]:off_o[j] + kv_c] for j in range(1, 7)))
            rows2 = lambda a: a.reshape(1, seq * n_kv_c, HD)
            by_group = lambda c: c.reshape(-1, n_kv_c, HD).transpose(1, 0, 2).astype(BF16)
            ck = by_group(nsa_compress(rows2(kc), *phi_k, n_kv_c))
            cv = by_group(nsa_compress(rows2(vc), *phi_v, n_kv_c))
            o_c = nsa_prompt(zp, off_o[0], off_o[9], dq_c // HD, ck, cv, ks.astype(BF16),
                             vs.astype(BF16), kw.astype(BF16), vw.astype(BF16))
            o_d, hist = conformer_conv(zp, off_o[7], off_o[8], conv_d_w[i], conv_d_b[i], ln_d_g[i],
                                       ln_d_b[i], jnp.zeros((_halo_rows(kw_d, 1), d_d), F32),
                                       tm=tm_mid, shift=1)
            st['p_conv_d'].append(hist[hist.shape[0] - (kw_d - 1):][None])
            yp = mm_norm_res([o_c, o_d], w_out_o, i, g, l, 1, yp, tm=tm_mid, tn=512)
            kv4 = lambda a: a.reshape(1, -1, n_kv_c, HD)
            for n, a in zip(('p_cmp_k', 'p_cmp_v', 'p_sel_k', 'p_sel_v'), (kc, vc, ks, vs)):
                st[n].append(kv4(a))
            st['p_win_k'].append(kv4(kw[seq - wb:]))
            st['p_win_v'].append(kv4(vw[seq - wb:]))
            q, kc, vc, ks, vs, kw, vw, _, _, gl = _col_splits(_bmajor(zs, nb), split_o)
            ck = nsa_compress_paged(cache_nsa_cmp_k[i], page_table, *phi_k)
            cv = nsa_compress_paged(cache_nsa_cmp_v[i], page_table, *phi_v)
            win_k, win_v = cache_nsa_win_k[i], cache_nsa_win_v[i]
            o_c = nsa_sample(q, gl, ck, cv, win_k, win_v, ks, vs, kw, vw,
                             cache_nsa_sel_k[i], cache_nsa_sel_v[i], page_table)
            o_d, hist = conformer_conv(zs, off_o[7], off_o[8], conv_d_w[i], conv_d_b[i], ln_d_g[i],
                                       ln_d_b[i], _tmajor(state_conv_d[i]), tm=ts, shift=nb)
            st['s_conv_d'].append(_bmajor(hist, nb))
            ys = mm_norm_res([_tmajor(o_c), o_d], w_out_o, i, g, l, 1, ys, tm=ts, tn=512)
            kv4 = lambda a: a.reshape(nb, nq, n_kv_c, HD)
            for n, a in zip(('s_cmp_k', 's_cmp_v', 's_sel_k', 's_sel_v'), (kc, vc, ks, vs)):
                st[n].append(kv4(a))
            st['s_win_k'].append(jnp.concatenate([win_k, kv4(kw)], axis=1)[:, nq:])
            st['s_win_v'].append(jnp.concatenate([win_v, kv4(vw)], axis=1)[:, nq:])

        n_mem = mem.shape[0]
        mk = norm_mm(mem, mem_g, l, 0, w_mk, l, tm=n_mem, tn=512)
        mv = norm_mm(mem, mem_g, l, 0, w_mv, l, tm=n_mem, tn=512)
        st['p_mem_k'].append(mk.reshape(1, n_mem, -1, HD))
        st['p_mem_v'].append(mv.reshape(1, n_mem, -1, HD))
        yp = cross_attn(yp, g, l, 2, 3, w_mq, w_mo, l, mk, mv, tm=tm_mid)
        qs = norm_mm(ys, g, l, 2, w_mq, l, tm=ts, tn=512)
        a = mem_attn_batched(_bmajor(qs, nb), cache_mem_k, cache_mem_v, l)
        ys = mm_norm_res([_tmajor(a)], w_mo, l, g, l, 3, ys, tm=ts, tn=512)

        kw_f = ffn_conv_w.shape[1]
        gact, hist = ffn_up(yp, g, l, 4, w_up, l, ffn_conv_w[l],
                            jnp.zeros((_halo_rows(kw_f, 1), d_ff), F32), tm=tm_big, shift=1)
        st['p_ffn'].append(hist[hist.shape[0] - (kw_f - 1):, :d_ff][None])
        yp = mm_norm_res([gact], w_down, l, g, l, 5, yp, tm=tm_mid, tn=512, widths=(d_ff,))
        gact, hist = ffn_up(ys, g, l, 4, w_up, l, ffn_conv_w[l], _tmajor(state_ffn_conv[l]),
                            tm=ts, shift=nb)
        st['s_ffn'].append(_bmajor(hist[:, :d_ff], nb))
        ys = mm_norm_res([gact], w_down, l, g, l, 5, ys, tm=ts, tn=512, widths=(d_ff,))

    out = {n: jnp.stack(a) for n, a in st.items()}
    return (yp.reshape(1, seq, d), _bmajor(ys, nb)) + tuple(out[n] for n in names)
```

```python
import functools

import jax
import jax.numpy as jnp
from jax import lax
from jax.experimental import pallas as pl
from jax.experimental.pallas import tpu as pltpu

F32 = jnp.float32
BF16 = jnp.bfloat16
I32 = jnp.int32

EPS = 1e-6
NEG = -1e30
BIG = 1e4
HD = 128
PAGE = 128
CMP_BLK = 64
N_SEL = 16
WINDOW = 512
TOPK_MAX = 256
H_IDX = 16
D_IDX = 64
INT_MIN = -(2 ** 31)

VMEM_LIMIT_BYTES = 56 * 1024 * 1024


def _params(*sem):
    return pltpu.CompilerParams(dimension_semantics=sem, vmem_limit_bytes=VMEM_LIMIT_BYTES)


def _rms(x, g):
    return x * lax.rsqrt(jnp.mean(x * x, axis=-1, keepdims=True) + EPS) * g


def _dot(a, b):
    return jnp.dot(a.astype(BF16), b.astype(BF16), preferred_element_type=F32)


def _dot_nt(a, b):
    return lax.dot_general(a.astype(BF16), b.astype(BF16), (((1,), (1,)), ((), ())),
                           preferred_element_type=F32)


def _gain_spec(g, layer):
    return pl.BlockSpec((None,) + g.shape[1:], lambda *_: (layer, 0, 0))


def _norm_mm_kernel(x_ref, g_ref, w_ref, o_ref, xn_ref, *, gr):
    @pl.when(pl.program_id(1) == 0)
    def _():
        xn_ref[...] = _rms(x_ref[...], g_ref[gr:gr + 1, :]).astype(BF16)

    o_ref[...] = jnp.dot(xn_ref[...], w_ref[...].astype(BF16), preferred_element_type=F32)


def norm_mm(x, g, gl, gr, w, wl, *, tm, tn):
    m, k = x.shape
    n = w.shape[2]
    return pl.pallas_call(
        functools.partial(_norm_mm_kernel, gr=gr),
        grid=(m // tm, pl.cdiv(n, tn)),
        in_specs=[pl.BlockSpec((tm, k), lambda i, j: (i, 0)),
                  _gain_spec(g, gl),
                  pl.BlockSpec((None, k, tn), lambda i, j: (wl, 0, j))],
        out_specs=pl.BlockSpec((tm, tn), lambda i, j: (i, j)),
        out_shape=jax.ShapeDtypeStruct((m, n), F32),
        scratch_shapes=[pltpu.VMEM((tm, k), BF16)],
        compiler_params=_params("arbitrary", "arbitrary"),
        name="norm_mm",
    )(x, g, w)


def _mm_norm_res_kernel(*refs, n_in, widths, tn, nj, gr):
    a_refs = refs[:n_in]
    w_ref, g_ref, y_ref, o_ref, a_bf = refs[n_in:]
    j = pl.program_id(1)

    @pl.when(j == 0)
    def _():
        off = 0
        for a_ref, wd in zip(a_refs, widths):
            a_bf[:, off:off + wd] = a_ref[...].astype(BF16)
            off += wd

    col = pl.multiple_of(j * tn, tn)
    o_ref[:, pl.ds(col, tn)] = jnp.dot(a_bf[...], w_ref[...].astype(BF16),
                                       preferred_element_type=F32)

    @pl.when(j == nj - 1)
    def _():
        o_ref[...] = y_ref[...] + _rms(o_ref[...], g_ref[gr:gr + 1, :])


def mm_norm_res(a_list, w, wl, g, gl, gr, y, *, tm, tn, widths=None):
    m, n = y.shape
    widths = tuple(a.shape[1] for a in a_list) if widths is None else tuple(widths)
    k = sum(widths)
    assert w.shape[1:] == (k, n) and n % tn == 0
    nj = n // tn
    kern = functools.partial(_mm_norm_res_kernel, n_in=len(a_list), widths=widths, tn=tn, nj=nj,
                             gr=gr)
    return pl.pallas_call(
        kern,
        grid=(m // tm, nj),
        in_specs=[pl.BlockSpec((tm, wd), lambda i, j: (i, 0)) for wd in widths]
        + [pl.BlockSpec((None, k, tn), lambda i, j: (wl, 0, j)),
           _gain_spec(g, gl),
           pl.BlockSpec((tm, n), lambda i, j: (i, 0))],
        out_specs=pl.BlockSpec((tm, n), lambda i, j: (i, 0)),
        out_shape=jax.ShapeDtypeStruct((m, n), F32),
        scratch_shapes=[pltpu.VMEM((tm, k), BF16)],
        compiler_params=_params("arbitrary", "arbitrary"),
        name="mm_norm_res",
    )(*a_list, w, g, y)


def _order_key(x):
    b = pltpu.bitcast(x, I32)
    return b ^ ((b >> 31) & 0x7FFFFFFF)


def _kth_largest_key(count_ge, k, shape):
    t0 = jnp.where(count_ge(jnp.zeros(shape, I32)) >= k, 0, INT_MIN).astype(I32)

    def bit_body(n, t):
        cand = t | jnp.left_shift(jnp.int32(1), 30 - n)
        return jnp.where(count_ge(cand) >= k, cand, t)

    return lax.fori_loop(0, 31, bit_body, t0)


LOG2E = 1.4426950408889634
M_INIT = 0.1 * NEG


def _stage_queries(q_ref, qg_sc, n_groups, rep, tq):
    c = HD ** -0.5 * LOG2E
    for g in range(n_groups):
        for r in range(rep):
            h = g * rep + r
            qg_sc[g, r * tq:(r + 1) * tq, 0:HD] = (q_ref[:, h * HD:(h + 1) * HD] * c).astype(BF16)


def _flash_step(qg, kc, vc, bias, carry):
    m, l, acc = carry
    s = _dot_nt(qg, kc)
    if bias is not None:
        s = s + bias
    m_new = jnp.maximum(m, jnp.max(s, axis=1, keepdims=True))
    p = jnp.exp2(s - m_new)
    alpha = jnp.exp2(m - m_new)
    return (m_new, alpha * l + jnp.sum(p, axis=1, keepdims=True),
            alpha * acc + jnp.dot(p.astype(BF16), vc, preferred_element_type=F32))


def _flash_init(rows):
    return (jnp.full((rows, 1), M_INIT, F32), jnp.zeros((rows, 1), F32), jnp.zeros((rows, HD), F32))


def _flash_finish(carry):
    _, l, acc = carry
    return acc / jnp.where(l > 0, l, 1.0)


def _dsa_prompt_kernel(q_ref, qia_ref, qib_ref, iw_ref, k_ref, v_ref, ik_ref, o_ref,
                       key_sc, qs_sc, wb_sc, qg_sc, *, tq, tks, tka, topk, n_heads, n_groups):
    i = pl.program_id(0)
    q0 = i * tq
    nch_a = (q0 + tq + tka - 1) // tka
    nch_s = nch_a * (tka // tks)
    rep = n_heads // n_groups
    qpos = q0 + lax.broadcasted_iota(I32, (tq, 1), 0)

    half = H_IDX // 2
    wscale = H_IDX ** -0.5 * D_IDX ** -0.5
    for h in range(H_IDX):
        src = qia_ref if h < half else qib_ref
        hh = h % half
        qs_sc[h * tq:(h + 1) * tq, :] = src[:, hh * D_IDX:(hh + 1) * D_IDX].astype(BF16)
        wb_sc[h] = jnp.broadcast_to(iw_ref[:, D_IDX + h:D_IDX + h + 1] * wscale, (tq, 128))
    _stage_queries(q_ref, qg_sc, n_groups, rep, tq)

    def score_body(c, carry):
        col = pl.multiple_of(c * tks, tks)
        ikc = ik_ref[pl.ds(col, tks), :]
        acc = jnp.zeros((tq, tks), F32)
        for h in range(H_IDX):
            s = lax.dot_general(qs_sc[h * tq:(h + 1) * tq, :], ikc, (((1,), (1,)), ((), ())),
                                preferred_element_type=F32)
            acc = acc + jnp.concatenate([wb_sc[h]] * (tks // 128), axis=1) * jnp.maximum(s, 0.0)
        kpos = col + lax.broadcasted_iota(I32, (1, tks), 1)
        key_sc[:, pl.ds(col, tks)] = jnp.where(kpos <= qpos, _order_key(acc), INT_MIN)
        return carry

    lax.fori_loop(0, nch_s, score_body, 0)

    def count_ge(t):
        tb = jnp.broadcast_to(t, (tq, 128))

        def body(c, cnt):
            col = pl.multiple_of(c * tka, tka)
            for j in range(tka // 128):
                cnt = cnt + jnp.where(key_sc[:, pl.ds(col + j * 128, 128)] >= tb, 1, 0)
            return cnt

        cnt = lax.fori_loop(0, nch_a, body, jnp.zeros((tq, 128), I32))
        return jnp.sum(cnt, axis=1, keepdims=True)

    thr = jnp.maximum(_kth_largest_key(count_ge, topk, (tq, 1)), INT_MIN + 1)

    def bias_body(c, carry):
        col = pl.multiple_of(c * tks, tks)
        bias = jnp.where(key_sc[:, pl.ds(col, tks)] >= thr, 0.0, NEG)
        key_sc[:, pl.ds(col, tks)] = pltpu.bitcast(bias, I32)
        return carry

    lax.fori_loop(0, nch_s, bias_body, 0)

    def att_body(c, carry):
        col = pl.multiple_of(c * tka, tka)
        bias = pltpu.bitcast(key_sc[:, pl.ds(col, tka)], F32)
        bias = jnp.concatenate([bias] * rep, axis=0)
        out = []
        for g in range(n_groups):
            kc = k_ref[pl.ds(col, tka), g * HD:(g + 1) * HD]
            vc = v_ref[pl.ds(col, tka), g * HD:(g + 1) * HD]
            out.append(_flash_step(qg_sc[g], kc, vc, bias, carry[g]))
        return tuple(out)

    res = lax.fori_loop(0, nch_a, att_body, tuple(_flash_init(rep * tq) for _ in range(n_groups)))
    for g in range(n_groups):
        o = _flash_finish(res[g])
        for r in range(rep):
            o_ref[:, (g * rep + r) * HD:(g * rep + r + 1) * HD] = o[r * tq:(r + 1) * tq, :]


def dsa_prompt(z, q_off, qi_off, iw_off, n_heads, k_bf, v_bf, ik_bf, *, tq=128, tks=256, tka=512):
    t = z.shape[0]
    dq = n_heads * HD
    half = H_IDX * D_IDX // 2
    n_groups = k_bf.shape[1] // HD
    assert q_off % dq == 0 and qi_off % half == 0 and iw_off % 128 == 0 and D_IDX + H_IDX <= 128
    assert t % tka == 0 and tka % tks == 0
    topk = min(TOPK_MAX, t // 4)
    kern = functools.partial(_dsa_prompt_kernel, tq=tq, tks=tks, tka=tka, topk=topk,
                             n_heads=n_heads, n_groups=n_groups)
    full = lambda a: pl.BlockSpec(a.shape, lambda i: (0, 0))
    col = lambda w, j: pl.BlockSpec((tq, w), lambda i: (i, j))
    return pl.pallas_call(
        kern,
        grid=(t // tq,),
        in_specs=[col(dq, q_off // dq), col(half, qi_off // half), col(half, qi_off // half + 1),
                  col(128, iw_off // 128), full(k_bf), full(v_bf), full(ik_bf)],
        out_specs=pl.BlockSpec((tq, dq), lambda i: (i, 0)),
        out_shape=jax.ShapeDtypeStruct((t, dq), F32),
        scratch_shapes=[pltpu.VMEM((tq, t), I32), pltpu.VMEM((H_IDX * tq, D_IDX), BF16),
                        pltpu.VMEM((H_IDX, tq, 128), F32),
                        pltpu.VMEM((n_groups, n_heads // n_groups * tq, HD), BF16)],
        compiler_params=_params("arbitrary"),
        name="dsa_prompt",
    )(z, z, z, z, k_bf, v_bf, ik_bf)


def _nsa_compress_kernel(x_ref, pe_ref, w1_ref, w2_ref, o_ref, *, nblk, n_groups):
    _compress_rows(x_ref, pe_ref, w1_ref, w2_ref, o_ref, nblk=nblk, n_groups=n_groups)


def _compress_rows(x_ref, pe_ref, w1_ref, w2_ref, o_ref, *, nblk, n_groups):
    acc = jnp.zeros((n_groups * nblk, HD), F32)
    for p in range(CMP_BLK):
        pe_p = pe_ref[p:p + 1, :]
        lhs = jnp.concatenate(
            [x_ref[pl.ds(p * n_groups + g, nblk, stride=CMP_BLK * n_groups), :] + pe_p
             for g in range(n_groups)], axis=0)
        acc = acc + _dot(lhs, w1_ref[p * HD:(p + 1) * HD, :])
    out = _dot(jnp.maximum(acc, 0.0), w2_ref[...])
    for g in range(n_groups):
        o_ref[:, g * HD:(g + 1) * HD] = out[g * nblk:(g + 1) * nblk, :]


def nsa_compress(x, pe, w1, w2, n_groups):
    b, tg, _ = x.shape
    nblk = tg // (n_groups * CMP_BLK)
    kern = functools.partial(_nsa_compress_kernel, nblk=nblk, n_groups=n_groups)
    full = lambda a: pl.BlockSpec(a.shape, lambda i: (0,) * a.ndim)
    return pl.pallas_call(
        kern,
        grid=(b,),
        in_specs=[pl.BlockSpec((None, tg, HD), lambda i: (i, 0, 0)), full(pe), full(w1), full(w2)],
        out_specs=pl.BlockSpec((None, nblk, n_groups * HD), lambda i: (i, 0, 0)),
        out_shape=jax.ShapeDtypeStruct((b, nblk, n_groups * HD), F32),
        compiler_params=_params("arbitrary"),
        name="nsa_compress",
    )(x, pe, w1, w2)


def _masked_softmax(s, ok):
    s = jnp.where(ok, s, NEG)
    e = jnp.where(ok, jnp.exp(s - jnp.max(s, axis=-1, keepdims=True)), 0.0)
    d = jnp.sum(e, axis=-1, keepdims=True)
    return e / jnp.where(d > 0, d, 1.0)


def _top_blocks(score, n_sel, axis=1):
    nb = score.shape[axis]
    pos = lax.broadcasted_iota(I32, score.shape, axis)
    sel = jnp.zeros(score.shape, F32)
    for _ in range(n_sel):
        m = jnp.max(score, axis=axis, keepdims=True)
        idx = jnp.min(jnp.where(score == m, pos, nb), axis=axis, keepdims=True)
        hit = pos == idx
        sel = jnp.where(hit, 1.0, sel)
        score = jnp.where(hit, -3e38, score)
    return sel


def _masked_softmax2(s, ok):
    s = jnp.where(ok, s, NEG)
    e = jnp.where(ok, jnp.exp2(s - jnp.max(s, axis=-1, keepdims=True)), 0.0)
    d = jnp.sum(e, axis=-1, keepdims=True)
    return e / jnp.where(d > 0, d, 1.0)


def _nsa_prompt_kernel(q_ref, gl_ref, ck_ref, cv_ref, ks_ref, vs_ref, kw_ref, vw_ref, o_ref,
                       qg_sc, *, tq, tk, n_heads, n_groups, nbc, gl_lane):
    i = pl.program_id(0)
    q0 = i * tq
    rep = n_heads // n_groups
    qpos = q0 + lax.broadcasted_iota(I32, (tq, 1), 0)
    jb = lax.broadcasted_iota(I32, (1, nbc), 1)
    cmp_ok = jnp.concatenate([jnp.where(((jb + 1) * CMP_BLK - 1) <= qpos, 1, 0)] * rep, axis=0) > 0
    cur = qpos // CMP_BLK
    gates = jax.nn.sigmoid(gl_ref[:, gl_lane:gl_lane + 3 * n_heads])

    _stage_queries(q_ref, qg_sc, n_groups, rep, tq)
    w0 = pl.multiple_of(jnp.maximum(q0 - WINDOW, 0), tq)
    wlen = WINDOW + tq
    diff = qpos - (w0 + lax.broadcasted_iota(I32, (1, wlen), 1))
    w_ok = jnp.where((diff >= 0) & (diff <= WINDOW), 1, 0)
    w_ok = jnp.concatenate([w_ok] * rep, axis=0) > 0
    imps = []
    for g in range(n_groups):
        qg = qg_sc[g, :, 0:HD]
        p_c = _masked_softmax2(_dot_nt(qg, ck_ref[g]), cmp_ok)
        o_c = _dot(p_c, cv_ref[g])
        imp = p_c[0:tq]
        for r in range(1, rep):
            imp = imp + p_c[r * tq:(r + 1) * tq]
        imps.append(imp)
        kw = kw_ref[pl.ds(w0, wlen), g * HD:(g + 1) * HD]
        vw = vw_ref[pl.ds(w0, wlen), g * HD:(g + 1) * HD]
        o_w = _dot(_masked_softmax2(_dot_nt(qg, kw), w_ok), vw)
        for r in range(rep):
            h = g * rep + r
            rows = slice(r * tq, (r + 1) * tq)
            o_ref[:, h * HD:(h + 1) * HD] = (gates[:, 3 * h:3 * h + 1] * o_c[rows]
                                             + gates[:, 3 * h + 2:3 * h + 3] * o_w[rows])

    imp_t = jnp.concatenate([imp.T for imp in imps], axis=1)
    jb_t = lax.broadcasted_iota(I32, (nbc, 1), 0)
    cur_t = (q0 + lax.broadcasted_iota(I32, (1, tq), 1)) // CMP_BLK
    cur_t = jnp.concatenate([cur_t] * n_groups, axis=1)
    blk_score = jnp.where(jb_t > cur_t, NEG, jnp.where((jb_t == cur_t) | (jb_t == 0), BIG, imp_t))
    sel_t = _top_blocks(blk_score, min(N_SEL, nbc), axis=0)
    for g in range(n_groups):
        pen = ((sel_t[:, g * tq:(g + 1) * tq].T - 1.0) * (-NEG)).astype(BF16)
        for r in range(rep):
            qg_sc[g, r * tq:(r + 1) * tq, HD:HD + nbc] = pen

    nch = (q0 + tq + tk - 1) // tk
    kw_aug = HD + nbc

    def chunk(c, carry, bias):
        col = pl.multiple_of(c * tk, tk)
        out = []
        for g in range(n_groups):
            kc = ks_ref[pl.ds(col, tk), g * kw_aug:(g + 1) * kw_aug]
            vc = vs_ref[pl.ds(col, tk), g * HD:(g + 1) * HD]
            out.append(_flash_step(qg_sc[g], kc, vc, bias, carry[g]))
        return tuple(out)

    init = tuple(_flash_init(rep * tq) for _ in range(n_groups))
    res = lax.fori_loop(0, nch - 1, lambda c, carry: chunk(c, carry, None), init)
    kpos = (nch - 1) * tk + lax.broadcasted_iota(I32, (1, tk), 1)
    causal = jnp.concatenate([jnp.where(kpos <= qpos, 0.0, NEG)] * rep, axis=0)
    res = chunk(nch - 1, res, causal)

    for g in range(n_groups):
        o_s = _flash_finish(res[g])
        for r in range(rep):
            h = g * rep + r
            o_ref[:, h * HD:(h + 1) * HD] += gates[:, 3 * h + 1:3 * h + 2] * o_s[r * tq:(r + 1) * tq]


def nsa_prompt(z, q_off, gl_off, n_heads, ck_bf, cv_bf, ks_bf, vs_bf, kw_bf, vw_bf, *, tq=128, tk=512):
    t = z.shape[0]
    dq = n_heads * HD
    n_groups, nbc, _ = ck_bf.shape
    gl_lane = gl_off % 128
    assert t >= WINDOW + tq and t % tk == 0 and nbc == t // CMP_BLK
    assert q_off % dq == 0 and gl_lane + 3 * n_heads <= 128
    kern = functools.partial(_nsa_prompt_kernel, tq=tq, tk=tk, n_heads=n_heads,
                             n_groups=n_groups, nbc=nbc, gl_lane=gl_lane)
    onehot = (jnp.arange(t, dtype=I32)[:, None] // CMP_BLK
              == jnp.arange(nbc, dtype=I32)[None, :]).astype(BF16)
    ks_aug = jnp.concatenate(
        [a for g in range(n_groups) for a in (ks_bf[:, g * HD:(g + 1) * HD], onehot)], axis=1)
    full = lambda a: pl.BlockSpec(a.shape, lambda i: (0,) * a.ndim)
    col = lambda w, j: pl.BlockSpec((tq, w), lambda i: (i, j))
    return pl.pallas_call(
        kern,
        grid=(t // tq,),
        in_specs=[col(dq, q_off // dq), col(128, gl_off // 128), full(ck_bf), full(cv_bf),
                  full(ks_aug), full(vs_bf), full(kw_bf), full(vw_bf)],
        out_specs=pl.BlockSpec((tq, dq), lambda i: (i, 0)),
        out_shape=jax.ShapeDtypeStruct((t, dq), F32),
        scratch_shapes=[pltpu.VMEM((n_groups, n_heads // n_groups * tq, HD + nbc), BF16)],
        compiler_params=_params("arbitrary"),
        name="nsa_prompt",
    )(z, z, ck_bf, cv_bf, ks_aug, vs_bf, kw_bf, vw_bf)


def _halo_rows(kw, shift):
    return max(8, (kw - 1) * shift)


SUBLANES = 8


def _dwconv(ext_ref, w_ref, u, prev_ref, first, *, tm, kw, shift, part_ref=None):
    hp = _halo_rows(kw, shift)
    base = hp - (kw - 1) * shift

    @pl.when(first)
    def _():
        ext_ref[0:hp, :] = prev_ref[...]
        if part_ref is not None:
            ext_ref[hp + tm:hp + tm + SUBLANES, :] = jnp.zeros((SUBLANES, ext_ref.shape[1]), F32)

    ext_ref[hp:hp + tm, :] = u
    y = None
    if part_ref is None:
        for i in range(kw):
            term = w_ref[i:i + 1, :] * ext_ref[pl.ds(base + i * shift, tm), :]
            y = term if y is None else y + term
    else:
        assert shift == 1 and base % SUBLANES == 0
        for b in range(min(SUBLANES, kw)):
            part = None
            for i in range(b, kw, SUBLANES):
                term = w_ref[i:i + 1, :] * ext_ref[pl.ds(base + i - b, tm + SUBLANES), :]
                part = term if part is None else part + term
            if b == 0:
                y = part[0:tm]
            else:
                part_ref[...] = part
                y = y + part_ref[pl.ds(b, tm), :]
    tail = ext_ref[tm:tm + hp, :]
    ext_ref[0:hp, :] = tail
    return y, tail


def _gated_conv_kernel(xa_ref, bg_ref, cg_ref, w_ref, prev_ref, o_ref, st_ref, ext_ref,
                       *, tm, kw, shift):
    u = cg_ref[...] * xa_ref[...]
    y, tail = _dwconv(ext_ref, w_ref, u, prev_ref, pl.program_id(0) == 0, tm=tm, kw=kw, shift=shift)
    o_ref[...] = bg_ref[...] * y
    st_ref[...] = tail


def gated_conv(z, c, w, prev, *, tm, shift):
    m = z.shape[0]
    kw = w.shape[0]
    hp = _halo_rows(kw, shift)
    kern = functools.partial(_gated_conv_kernel, tm=tm, kw=kw, shift=shift)
    col = lambda j: pl.BlockSpec((tm, c), lambda i: (i, j))
    return pl.pallas_call(
        kern,
        grid=(m // tm,),
        in_specs=[col(0), col(1), col(2), pl.BlockSpec((kw, c), lambda i: (0, 0)),
                  pl.BlockSpec((hp, c), lambda i: (0, 0))],
        out_specs=[pl.BlockSpec((tm, c), lambda i: (i, 0)), pl.BlockSpec((hp, c), lambda i: (0, 0))],
        out_shape=[jax.ShapeDtypeStruct((m, c), F32), jax.ShapeDtypeStruct((hp, c), F32)],
        scratch_shapes=[pltpu.VMEM((hp + tm, c), F32)],
        compiler_params=_params("arbitrary"),
        name="gated_conv",
    )(z, z, z, w, prev)


def _conformer_kernel(dpa_ref, dpb_ref, dga_ref, dgb_ref, w_ref, b_ref, lg_ref, lb_ref, prev_ref,
                      o_ref, st_ref, ext_ref, *part_ref, tm, kw, shift):
    u = jnp.concatenate([dpa_ref[...] * jax.nn.sigmoid(dga_ref[...]),
                         dpb_ref[...] * jax.nn.sigmoid(dgb_ref[...])], axis=1)
    c, tail = _dwconv(ext_ref, w_ref, u, prev_ref, pl.program_id(0) == 0, tm=tm, kw=kw, shift=shift,
                      part_ref=part_ref[0] if part_ref else None)
    c = c + b_ref[...]
    mu = jnp.mean(c, axis=-1, keepdims=True)
    xc = c - mu
    y = xc * lax.rsqrt(jnp.mean(xc * xc, axis=-1, keepdims=True) + EPS) * lg_ref[...] + lb_ref[...]
    o_ref[...] = y * jax.nn.sigmoid(y)
    st_ref[...] = tail


def conformer_conv(z, dp_off, dg_off, w, b, ln_g, ln_b, prev, *, tm, shift):
    m = z.shape[0]
    kw, c = w.shape
    half = c // 2
    assert dp_off % half == 0 and dg_off % half == 0
    hp = _halo_rows(kw, shift)
    kern = functools.partial(_conformer_kernel, tm=tm, kw=kw, shift=shift)
    col = lambda off, j: pl.BlockSpec((tm, half), lambda i: (i, off // half + j))
    row = pl.BlockSpec((tm, c), lambda i: (i, 0))
    vec = pl.BlockSpec((1, c), lambda i: (0, 0))
    grouped = shift == 1 and (hp - (kw - 1)) % SUBLANES == 0
    scratch = ([pltpu.VMEM((hp + tm + SUBLANES, c), F32), pltpu.VMEM((tm + SUBLANES, c), F32)]
               if grouped else [pltpu.VMEM((hp + tm, c), F32)])
    return pl.pallas_call(
        kern,
        grid=(m // tm,),
        in_specs=[col(dp_off, 0), col(dp_off, 1), col(dg_off, 0), col(dg_off, 1),
                  pl.BlockSpec((kw, c), lambda i: (0, 0)), vec, vec, vec,
                  pl.BlockSpec((hp, c), lambda i: (0, 0))],
        out_specs=[row, pl.BlockSpec((hp, c), lambda i: (0, 0))],
        out_shape=[jax.ShapeDtypeStruct((m, c), F32), jax.ShapeDtypeStruct((hp, c), F32)],
        scratch_shapes=scratch,
        compiler_params=_params("arbitrary"),
        name="conformer_conv",
    )(z, z, z, z, w, b.reshape(1, c), ln_g.reshape(1, c), ln_b.reshape(1, c), prev)


FFN_TF = 128
FFN_NB = 4


def _ffn_up_kernel(*refs, tm, kw, shift, gr):
    y_ref, g_ref = refs[:2]
    wa_refs = refs[2:2 + FFN_NB]
    wv_refs = refs[2 + FFN_NB:2 + 2 * FFN_NB]
    cw_ref, prev_ref, o_ref, st_ref, xn_ref, wcat_ref, halo_ref, ext_ref = refs[2 + 2 * FFN_NB:]
    i = pl.program_id(0)
    j = pl.program_id(1)
    hp = _halo_rows(kw, shift)
    tf = FFN_NB * FFN_TF

    @pl.when(j == 0)
    def _():
        xn_ref[...] = _rms(y_ref[...], g_ref[gr:gr + 1, :]).astype(BF16)

    @pl.when(i == 0)
    def _():
        ext_ref[0:hp, :] = prev_ref[...]

    @pl.when(i > 0)
    def _():
        ext_ref[0:hp, :] = halo_ref[j]

    def granule_matmul(s):
        wcat_ref[s, :, 0:FFN_TF] = wa_refs[s][...].astype(BF16)
        wcat_ref[s, :, FFN_TF:2 * FFN_TF] = wv_refs[s][...].astype(BF16)
        return jnp.dot(xn_ref[...], wcat_ref[s], preferred_element_type=F32)

    h_next = granule_matmul(0)
    for s in range(FFN_NB):
        cols = slice(s * FFN_TF, (s + 1) * FFN_TF)
        h = h_next
        if s + 1 < FFN_NB:
            h_next = granule_matmul(s + 1)
        ext_ref[hp:hp + tm, cols] = h[:, 0:FFN_TF]
        c = None
        for t in range(kw):
            term = cw_ref[t:t + 1, cols] * ext_ref[pl.ds(hp - (kw - 1 - t) * shift, tm), cols]
            c = term if c is None else c + term
        o_ref[:, cols] = (c * jax.nn.sigmoid(c) * h[:, FFN_TF:2 * FFN_TF]).astype(BF16)
    tail = ext_ref[tm:tm + hp, :]
    halo_ref[j] = tail
    st_ref[...] = tail


def ffn_up(y, g, gl, gr, w_up, wl, conv_w, prev, *, tm, shift):
    m, d = y.shape
    kw, f = conv_w.shape
    nf = f // FFN_TF
    assert f % FFN_TF == 0 and w_up.shape[1:] == (d, 2 * f)
    tf = FFN_NB * FFN_TF
    nj = pl.cdiv(nf, FFN_NB)
    fp = nj * tf
    hp = _halo_rows(kw, shift)
    nm = m // tm
    conv_w = jnp.pad(conv_w, ((0, 0), (0, fp - f)))
    prev = jnp.pad(prev, ((0, 0), (0, fp - f)))
    kern = functools.partial(_ffn_up_kernel, tm=tm, kw=kw, shift=shift, gr=gr)

    def granule(base, s):
        return pl.BlockSpec((None, d, FFN_TF),
                            lambda i, j: (wl, 0, base + jnp.minimum(FFN_NB * j + s, nf - 1)))

    return pl.pallas_call(
        kern,
        grid=(nm, nj),
        in_specs=[pl.BlockSpec((tm, d), lambda i, j: (i, 0)), _gain_spec(g, gl)]
        + [granule(0, s) for s in range(FFN_NB)] + [granule(nf, s) for s in range(FFN_NB)]
        + [pl.BlockSpec((kw, tf), lambda i, j: (0, j)), pl.BlockSpec((hp, tf), lambda i, j: (0, j))],
        out_specs=[pl.BlockSpec((tm, tf), lambda i, j: (i, j)),
                   pl.BlockSpec((hp, tf), lambda i, j: (i, j))],
        out_shape=[jax.ShapeDtypeStruct((m, fp), BF16), jax.ShapeDtypeStruct((nm * hp, fp), F32)],
        scratch_shapes=[pltpu.VMEM((tm, d), BF16), pltpu.VMEM((FFN_NB, d, 2 * FFN_TF), BF16),
                        pltpu.VMEM((nj, hp, tf), F32), pltpu.VMEM((hp + tm, tf), F32)],
        compiler_params=_params("arbitrary", "arbitrary"),
        name="ffn_up",
    )(y, g, *([w_up] * (2 * FFN_NB)), conv_w, prev)


def _mem_heads(q, mk_head, mv_head, n_heads):
    scale = HD ** -0.5
    outs = []
    for h in range(n_heads):
        s = _dot_nt(q[:, h * HD:(h + 1) * HD], mk_head(h)) * scale
        e = jnp.exp(s - jnp.max(s, axis=-1, keepdims=True))
        p = e / jnp.sum(e, axis=-1, keepdims=True)
        outs.append(_dot(p, mv_head(h)))
    return jnp.concatenate(outs, axis=1)


def _xattn_kernel(y_ref, g_ref, wq_ref, mk_ref, mv_ref, wo_ref, o_ref, wq_bf, wo_bf,
                  *, n_heads, gr_in, gr_out):
    @pl.when(pl.program_id(0) == 0)
    def _():
        wq_bf[...] = wq_ref[...].astype(BF16)
        wo_bf[...] = wo_ref[...].astype(BF16)

    y = y_ref[...]
    q = jnp.dot(_rms(y, g_ref[gr_in:gr_in + 1, :]).astype(BF16), wq_bf[...],
                preferred_element_type=F32)
    o = _mem_heads(q, lambda h: mk_ref[:, h * HD:(h + 1) * HD],
                   lambda h: mv_ref[:, h * HD:(h + 1) * HD], n_heads)
    f = jnp.dot(o.astype(BF16), wo_bf[...], preferred_element_type=F32)
    o_ref[...] = y + _rms(f, g_ref[gr_out:gr_out + 1, :])


def cross_attn(y, g, gl, gr_in, gr_out, wq, wo, wl, mk, mv, *, tm):
    m, d = y.shape
    dh = wq.shape[2]
    kern = functools.partial(_xattn_kernel, n_heads=dh // HD, gr_in=gr_in, gr_out=gr_out)
    full = lambda a: pl.BlockSpec(a.shape, lambda i: (0, 0))
    layer = lambda a: pl.BlockSpec((None,) + a.shape[1:], lambda i: (wl, 0, 0))
    row = pl.BlockSpec((tm, d), lambda i: (i, 0))
    return pl.pallas_call(
        kern,
        grid=(m // tm,),
        in_specs=[row, _gain_spec(g, gl), layer(wq), full(mk), full(mv), layer(wo)],
        out_specs=row,
        out_shape=jax.ShapeDtypeStruct((m, d), F32),
        scratch_shapes=[pltpu.VMEM(wq.shape[1:], BF16), pltpu.VMEM(wo.shape[1:], BF16)],
        compiler_params=_params("arbitrary"),
        name="cross_attn",
    )(y, g, wq, mk, mv, wo)


def _mem_attn_kernel(q_ref, mk_ref, mv_ref, o_ref, *, n_heads):
    o_ref[...] = _mem_heads(q_ref[...], lambda h: mk_ref[:, h, :], lambda h: mv_ref[:, h, :], n_heads)


def mem_attn_batched(q, mk, mv, layer):
    b, t, dh = q.shape
    kern = functools.partial(_mem_attn_kernel, n_heads=dh // HD)
    cache = pl.BlockSpec((None, None) + mk.shape[2:], lambda i: (layer, i, 0, 0, 0))
    blk = pl.BlockSpec((None, t, dh), lambda i: (i, 0, 0))
    return pl.pallas_call(
        kern,
        grid=(b,),
        in_specs=[blk, cache, cache],
        out_specs=blk,
        out_shape=jax.ShapeDtypeStruct(q.shape, F32),
        compiler_params=_params("arbitrary"),
        name="mem_attn_batched",
    )(q, mk, mv)


def _page_copy(cache, pt_ref, b, p, buf, slot, sem):
    rows = cache.shape[1]
    return pltpu.make_async_copy(cache.at[pt_ref[b, p]],
                                 buf.at[slot, pl.ds(p * rows, rows), :], sem.at[slot])


def _for_pages(caches, bufs, sems, pt_ref, b, slot, n_pages, fn):
    def body(p, carry):
        for cache, buf, sem in zip(caches, bufs, sems):
            fn(_page_copy(cache, pt_ref, b, p, buf, slot, sem))
        return carry

    lax.fori_loop(0, n_pages, body, 0)


def _fetch_pages(caches, bufs, sems, pt_ref, n_pages):
    b = pl.program_id(0)
    slot = b % 2
    args = (caches, bufs, sems, pt_ref)

    @pl.when(b == 0)
    def _():
        _for_pages(*args, b, slot, n_pages, lambda cp: cp.start())

    @pl.when(b + 1 < pl.num_programs(0))
    def _():
        _for_pages(*args, b + 1, 1 - slot, n_pages, lambda cp: cp.start())

    _for_pages(*args, b, slot, n_pages, lambda cp: cp.wait())
    return slot


def _paged_call(kern, pt, inputs, n_any, out_shape, out_block, scratch, name):
    nb = pt.shape[0]

    def spec(a):
        nd = a.ndim - 1
        return pl.BlockSpec((None,) + a.shape[1:], lambda i, pt_ref: (i,) + (0,) * nd)

    def shared(a):
        nd = a.ndim
        return pl.BlockSpec(a.shape, lambda i, pt_ref: (0,) * nd)

    in_specs = []
    for a in inputs[:len(inputs) - n_any]:
        in_specs.append(spec(a) if a.shape[0] == nb and a.ndim >= 3 else shared(a))
    in_specs += [pl.BlockSpec(memory_space=pl.ANY)] * n_any
    nd_o = len(out_block)
    return pl.pallas_call(
        kern,
        grid_spec=pltpu.PrefetchScalarGridSpec(
            num_scalar_prefetch=1,
            grid=(nb,),
            in_specs=in_specs,
            out_specs=pl.BlockSpec((None,) + tuple(out_block), lambda i, pt_ref: (i,) + (0,) * nd_o),
            scratch_shapes=scratch),
        out_shape=jax.ShapeDtypeStruct((nb,) + tuple(out_block), out_shape),
        compiler_params=_params("arbitrary"),
        name=name,
    )(pt, *inputs)


def _attend_two_parts(qg, k_a, v_a, ok_a, k_b, v_b, ok_b):
    scale = HD ** -0.5
    s_a = jnp.where(ok_a, _dot_nt(qg, k_a) * scale, NEG)
    s_b = jnp.where(ok_b, _dot_nt(qg, k_b) * scale, NEG)
    mx = jnp.maximum(jnp.max(s_a, axis=1, keepdims=True), jnp.max(s_b, axis=1, keepdims=True))
    e_a = jnp.where(ok_a, jnp.exp(s_a - mx), 0.0)
    e_b = jnp.where(ok_b, jnp.exp(s_b - mx), 0.0)
    den = jnp.sum(e_a, axis=1, keepdims=True) + jnp.sum(e_b, axis=1, keepdims=True)
    return (_dot(e_a, v_a) + _dot(e_b, v_b)) / jnp.where(den > 0, den, 1.0)


def _dsa_sample_select_kernel(pt_ref, qi_ref, wi_ref, ikn_ref, ik_hbm, m_ref, ikbuf, key_sc, sem,
                              *, n_pages, tk, topk, n_new, nq, rows):
    slot = _fetch_pages([ik_hbm], [ikbuf], [sem], pt_ref, n_pages)
    past = n_pages * PAGE
    qi = qi_ref[...].astype(BF16)
    w = wi_ref[...] * (H_IDX ** -0.5 * D_IDX ** -0.5)

    def scores(ikc):
        r = jnp.maximum(_dot_nt(qi, ikc), 0.0) * w
        return jnp.sum(r.reshape(H_IDX, rows, ikc.shape[0]), axis=0)

    def score_body(c, carry):
        col = pl.multiple_of(c * tk, tk)
        key_sc[:, pl.ds(col, tk)] = _order_key(scores(ikbuf[slot, pl.ds(col, tk), :]))
        return carry

    lax.fori_loop(0, past // tk, score_body, 0)
    t_row = lax.rem(lax.broadcasted_iota(I32, (rows, PAGE), 0), nq)
    col = lax.broadcasted_iota(I32, (rows, PAGE), 1)
    new_ok = (col < n_new) & (col <= t_row)
    key_sc[:, past:past + PAGE] = jnp.where(new_ok, _order_key(scores(ikn_ref[...])), INT_MIN)

    def count_ge(t):
        return jnp.sum(jnp.where(key_sc[...] >= t, 1, 0), axis=1, keepdims=True)

    thr = jnp.maximum(_kth_largest_key(count_ge, topk, (rows, 1)), INT_MIN + 1)
    m_ref[...] = jnp.where(key_sc[...] >= thr, 1.0, 0.0)


def _dsa_sample_attn_kernel(pt_ref, q_ref, m_ref, kn_ref, vn_ref, k_hbm, v_hbm, o_ref,
                            kbuf, vbuf, ksem, vsem, *, n_pages, n_groups):
    slot = _fetch_pages([k_hbm, v_hbm], [kbuf, vbuf], [ksem, vsem], pt_ref, n_pages)
    past = n_pages * PAGE
    tile = q_ref.shape[1] // m_ref.shape[0]
    ok_c = jnp.concatenate([m_ref[:, 0:past]] * tile, axis=0) > 0.5
    ok_n = jnp.concatenate([m_ref[:, past:past + PAGE]] * tile, axis=0) > 0.5
    for g in range(n_groups):
        o_ref[g] = _attend_two_parts(
            q_ref[g].astype(BF16),
            kbuf[slot, pl.ds(g, past, stride=n_groups), :],
            vbuf[slot, pl.ds(g, past, stride=n_groups), :], ok_c,
            kn_ref[pl.ds(g, PAGE, stride=n_groups), :],
            vn_ref[pl.ds(g, PAGE, stride=n_groups), :], ok_n)


def _group_rows(x, n_groups):
    b, t, dq = x.shape
    rep = dq // HD // n_groups
    return x.reshape(b, t, n_groups, rep, HD).transpose(0, 2, 3, 1, 4).reshape(b, n_groups, rep * t, HD)


def _ungroup_rows(o, t):
    b, g, rows, _ = o.shape
    rep = rows // t
    return o.reshape(b, g, rep, t, HD).transpose(0, 3, 1, 2, 4).reshape(b, t, g * rep * HD)


def _pad_new_rows(x, n_groups):
    b, t, _ = x.shape
    x = x.reshape(b, t * n_groups, HD)
    return jnp.pad(x, ((0, 0), (0, (PAGE - t) * n_groups), (0, 0)))


def dsa_sample(q, qi, wi, k_new, v_new, ik_new, cache_k, cache_v, cache_ik, pt, *, tk=1024):
    nb, nq, dq = q.shape
    n_pages = pt.shape[1]
    n_groups = cache_k.shape[2]
    rep = dq // HD // n_groups
    rows = rep * nq
    past = n_pages * PAGE
    topk = min(TOPK_MAX, (past + nq) // 4)
    width = past + PAGE
    rep_s = max(1, 8 // nq)
    rows_s = rep_s * nq
    assert rows % rows_s == 0
    qi_r = jnp.broadcast_to(qi.reshape(nb, nq, H_IDX, D_IDX).transpose(0, 2, 1, 3)[:, :, None],
                            (nb, H_IDX, rep_s, nq, D_IDX)).reshape(nb, H_IDX * rows_s, D_IDX)
    wi_r = jnp.broadcast_to(wi.transpose(0, 2, 1)[:, :, None],
                            (nb, H_IDX, rep_s, nq)).reshape(nb, H_IDX * rows_s, 1)
    ikn = jnp.pad(ik_new, ((0, 0), (0, PAGE - nq), (0, 0)))
    sel_kern = functools.partial(_dsa_sample_select_kernel, n_pages=n_pages, tk=tk, topk=topk,
                                 n_new=nq, nq=nq, rows=rows_s)
    mask = _paged_call(
        sel_kern, pt, [qi_r, wi_r, ikn, cache_ik], 1, F32, (rows_s, width),
        [pltpu.VMEM((2, past, D_IDX), F32), pltpu.VMEM((rows_s, width), I32),
         pltpu.SemaphoreType.DMA((2,))], "dsa_sample_select")
    ck = cache_k.reshape(cache_k.shape[0], PAGE * n_groups, HD)
    cv = cache_v.reshape(cache_v.shape[0], PAGE * n_groups, HD)
    att_kern = functools.partial(_dsa_sample_attn_kernel, n_pages=n_pages, n_groups=n_groups)
    o = _paged_call(
        att_kern, pt, [_group_rows(q, n_groups), mask, _pad_new_rows(k_new, n_groups),
                       _pad_new_rows(v_new, n_groups), ck, cv], 2, F32, (n_groups, rows, HD),
        [pltpu.VMEM((2, past * n_groups, HD), F32), pltpu.VMEM((2, past * n_groups, HD), F32),
         pltpu.SemaphoreType.DMA((2,)), pltpu.SemaphoreType.DMA((2,))], "dsa_sample_attn")
    return _ungroup_rows(o, nq)


def _nsa_compress_paged_kernel(pt_ref, pe_ref, w1_ref, w2_ref, x_hbm, o_ref, buf, sem,
                               *, n_pages, n_groups):
    slot = _fetch_pages([x_hbm], [buf], [sem], pt_ref, n_pages)
    _compress_rows(buf.at[slot], pe_ref, w1_ref, w2_ref, o_ref,
                   nblk=n_pages * PAGE // CMP_BLK, n_groups=n_groups)


def nsa_compress_paged(cache, pt, pe, w1, w2):
    n_groups = cache.shape[2]
    n_pages = pt.shape[1]
    past = n_pages * PAGE
    kern = functools.partial(_nsa_compress_paged_kernel, n_pages=n_pages, n_groups=n_groups)
    view = cache.reshape(cache.shape[0], PAGE * n_groups, HD)
    return _paged_call(
        kern, pt, [pe, w1.astype(BF16), w2, view], 1, F32, (past // CMP_BLK, n_groups * HD),
        [pltpu.VMEM((2, past * n_groups, HD), F32), pltpu.SemaphoreType.DMA((2,))],
        "nsa_compress_paged")


def _nsa_sample_kernel(pt_ref, q_ref, gl_ref, ck_ref, cv_ref, ex_ref, wk_ref, wv_ref, skn_ref, svn_ref,
                       wkn_ref, wvn_ref, ks_hbm, vs_hbm, o_ref, kbuf, vbuf, ksem, vsem,
                       *, n_pages, n_groups, nq, n_new):
    slot = _fetch_pages([ks_hbm, vs_hbm], [kbuf, vbuf], [ksem, vsem], pt_ref, n_pages)
    past = n_pages * PAGE
    scale = HD ** -0.5
    rows = q_ref.shape[1]
    rep = rows // nq
    nbc = ck_ref.shape[0]
    wlen = wk_ref.shape[0] // n_groups
    t_row = lax.rem(lax.broadcasted_iota(I32, (rows, 1), 0), nq)
    qpos = past + t_row
    jb = lax.broadcasted_iota(I32, (1, nbc), 1)
    cmp_ok = ((jb + 1) * CMP_BLK - 1) <= qpos
    cur = qpos // CMP_BLK
    newcol = lax.broadcasted_iota(I32, (1, PAGE), 1)
    new_ok = (newcol < n_new) & (newcol <= t_row)
    diff = qpos - (past - wlen + lax.broadcasted_iota(I32, (1, wlen), 1))
    win_ok = (diff >= 0) & (diff <= WINDOW)
    o_cs, scores = [], []
    for g in range(n_groups):
        cols = slice(g * HD, (g + 1) * HD)
        p_c = _masked_softmax(_dot_nt(q_ref[g].astype(BF16), ck_ref[:, cols]) * scale, cmp_ok)
        o_cs.append(_dot(p_c, cv_ref[:, cols]))
        imp = p_c
        for r in range(1, rep):
            imp = imp + pltpu.roll(p_c, r * nq, 0)
        scores.append(jnp.where(jb > cur, NEG, jnp.where((jb == cur) | (jb == 0), BIG, imp)))
    selm = _top_blocks(jnp.concatenate(scores, axis=0).T, N_SEL - 1, axis=0).T.astype(BF16)
    picked_all = jnp.dot(selm, ex_ref[...], preferred_element_type=F32)
    for g in range(n_groups):
        qg = q_ref[g].astype(BF16)
        gates = jax.nn.sigmoid(gl_ref[g])
        o_c = o_cs[g]
        picked = picked_all[g * rows:(g + 1) * rows] > 0.5
        o_s = _attend_two_parts(
            qg, kbuf[slot, pl.ds(g, past, stride=n_groups), :],
            vbuf[slot, pl.ds(g, past, stride=n_groups), :], picked,
            skn_ref[pl.ds(g, PAGE, stride=n_groups), :],
            svn_ref[pl.ds(g, PAGE, stride=n_groups), :], new_ok)
        o_w = _attend_two_parts(
            qg, wk_ref[pl.ds(g, wlen, stride=n_groups), :],
            wv_ref[pl.ds(g, wlen, stride=n_groups), :], win_ok,
            wkn_ref[pl.ds(g, PAGE, stride=n_groups), :],
            wvn_ref[pl.ds(g, PAGE, stride=n_groups), :], new_ok)
        o_ref[g] = gates[:, 0:1] * o_c + gates[:, 1:2] * o_s + gates[:, 2:3] * o_w


def nsa_sample(q, gl, ck, cv, win_k, win_v, ks_new, vs_new, kw_new, vw_new, cache_ks, cache_vs, pt):
    nb, nq, dq = q.shape
    n_pages = pt.shape[1]
    n_groups = cache_ks.shape[2]
    n_heads = dq // HD
    rep = n_heads // n_groups
    rows = rep * nq
    past = n_pages * PAGE
    assert past % CMP_BLK == 0 and nq <= CMP_BLK and ck.shape[1] == past // CMP_BLK
    assert win_k.shape[1] == WINDOW
    glr = gl.reshape(nb, nq, n_groups, rep, 3).transpose(0, 2, 3, 1, 4).reshape(nb, n_groups, rows, 3)
    view = lambda c: c.reshape(c.shape[0], c.shape[1] * n_groups, HD)
    kern = functools.partial(_nsa_sample_kernel, n_pages=n_pages, n_groups=n_groups,
                             nq=nq, n_new=nq)
    pad = lambda x: _pad_new_rows(x, n_groups)
    expand = (jnp.arange(past, dtype=I32)[None, :] // CMP_BLK
              == jnp.arange(past // CMP_BLK, dtype=I32)[:, None]).astype(BF16)
    o = _paged_call(
        kern, pt, [_group_rows(q, n_groups), glr, ck, cv, expand, view(win_k), view(win_v),
                   pad(ks_new), pad(vs_new), pad(kw_new), pad(vw_new), view(cache_ks), view(cache_vs)],
        2, F32, (n_groups, rows, HD),
        [pltpu.VMEM((2, past * n_groups, HD), F32), pltpu.VMEM((2, past * n_groups, HD), F32),
         pltpu.SemaphoreType.DMA((2,)), pltpu.SemaphoreType.DMA((2,))], "nsa_sample")
    return _ungroup_rows(o, nq)


def _tmajor(a):
    b, t, c = a.shape
    return jnp.swapaxes(a, 0, 1).reshape(t * b, c)


def _bmajor(a, b):
    tb, c = a.shape
    return jnp.swapaxes(a.reshape(tb // b, b, c), 0, 1)


def _col_splits(z, sizes):
    out, off = [], 0
    for s in sizes:
        out.append(z[..., off:off + s])
        off += s
    return out


def _row_tile(m, pref):
    return pref if m % pref == 0 else m


def kernel(x_prompt, x_sample, state_conv_a, cache_dsa_k, cache_dsa_v, cache_dsa_ik, cache_nsa_cmp_k, cache_nsa_cmp_v, cache_nsa_sel_k, cache_nsa_sel_v, cache_nsa_win_k, cache_nsa_win_v, state_conv_d, state_ffn_conv, cache_mem_k, cache_mem_v, page_table, mem_prompt, w_in_e, conv_a_w, w_out_e, w_in_o, nsa_pe_k, nsa_w1_k, nsa_w2_k, nsa_pe_v, nsa_w1_v, nsa_w2_v, conv_d_w, conv_d_b, ln_d_g, ln_d_b, w_out_o, norm_g, mem_norm_g, w_mq, w_mk, w_mv, w_mo, w_up, ffn_conv_w, w_down):
    bp, seq, d = x_prompt.shape
    nb, nq, _ = x_sample.shape
    assert bp == 1
    depth = norm_g.shape[0]
    ts = nb * nq
    d_a = conv_a_w.shape[-1]
    d_d = conv_d_w.shape[-1]
    d_ff = ffn_conv_w.shape[-1]
    kv_b = cache_dsa_k.shape[-2] * HD
    kv_c = cache_nsa_cmp_k.shape[-2] * HD
    n_kv_c = cache_nsa_cmp_k.shape[-2]
    dq_b = w_out_e.shape[1] - d_a
    dq_c = w_out_o.shape[1] - d_d
    n_gate = 3 * dq_c // HD
    split_e = (d_a, d_a, d_a, dq_b, kv_b, kv_b, H_IDX * D_IDX, D_IDX, H_IDX)
    split_o = (dq_c,) + (kv_c,) * 6 + (d_d, d_d, n_gate)
    gl_src = dq_c + 6 * kv_c
    w_in_o = jnp.concatenate([w_in_o[:, :, :gl_src], w_in_o[:, :, gl_src + n_gate:],
                              w_in_o[:, :, gl_src:gl_src + n_gate]], axis=2).astype(BF16)
    w_in_e, w_out_e, w_out_o, w_mq, w_mo, w_down = (
        w.astype(BF16) for w in (w_in_e, w_out_e, w_out_o, w_mq, w_mo, w_down))
    off_e = [sum(split_e[:j]) for j in range(len(split_e))]
    off_o = [sum(split_o[:j]) for j in range(len(split_o))]
    mem_g = mem_norm_g.reshape(depth, 1, d)
    tm_big = _row_tile(seq, 1024)
    tm_mid = _row_tile(seq, 512)
    wb = min(WINDOW, seq)

    yp = x_prompt.reshape(seq, d)
    ys = _tmajor(x_sample)
    mem = mem_prompt.reshape(mem_prompt.shape[1], d)
    names = ('p_conv_a', 'p_dsa_k', 'p_dsa_v', 'p_dsa_ik', 'p_cmp_k', 'p_cmp_v', 'p_sel_k', 'p_sel_v',
             'p_win_k', 'p_win_v', 'p_conv_d', 'p_ffn', 'p_mem_k', 'p_mem_v',
             's_conv_a', 's_dsa_k', 's_dsa_v', 's_dsa_ik', 's_cmp_k', 's_cmp_v', 's_sel_k', 's_sel_v',
             's_win_k', 's_win_v', 's_conv_d', 's_ffn')
    st = {n: [] for n in names}

    for l in range(depth):
        g = norm_g
        i = l // 2
        if l % 2 == 0:
            zp = norm_mm(yp, g, l, 0, w_in_e, i, tm=tm_big, tn=1024)
            zs = norm_mm(ys, g, l, 0, w_in_e, i, tm=ts, tn=512)
            kw_a = conv_a_w.shape[1]
            o_a, hist = gated_conv(zp, d_a, conv_a_w[i], jnp.zeros((_halo_rows(kw_a, 1), d_a), F32),
                                   tm=tm_mid, shift=1)
            st['p_conv_a'].append(hist[hist.shape[0] - (kw_a - 1):][None])
            k, v, ik = lax.optimization_barrier(
                tuple(zp[:, off_e[j]:off_e[j] + split_e[j]] for j in (4, 5, 7)))
            o_b = dsa_prompt(zp, off_e[3], off_e[6], off_e[7], dq_b // HD,
                             k.astype(BF16), v.astype(BF16), ik.astype(BF16))
            yp = mm_norm_res([o_a, o_b], w_out_e, i, g, l, 1, yp, tm=tm_mid, tn=512)
            st['p_dsa_k'].append(k.reshape(1, seq, -1, HD))
            st['p_dsa_v'].append(v.reshape(1, seq, -1, HD))
            st['p_dsa_ik'].append(ik[None])
            o_a, hist = gated_conv(zs, d_a, conv_a_w[i], _tmajor(state_conv_a[i]), tm=ts, shift=nb)
            st['s_conv_a'].append(_bmajor(hist, nb))
            _, _, _, q, k, v, qi, ik, wi = _col_splits(_bmajor(zs, nb), split_e)
            o_b = dsa_sample(q, qi, wi, k, v, ik, cache_dsa_k[i], cache_dsa_v[i], cache_dsa_ik[i],
                             page_table)
            ys = mm_norm_res([o_a, _tmajor(o_b)], w_out_e, i, g, l, 1, ys, tm=ts, tn=512)
            st['s_dsa_k'].append(k.reshape(nb, nq, -1, HD))
            st['s_dsa_v'].append(v.reshape(nb, nq, -1, HD))
            st['s_dsa_ik'].append(ik)
        else:
            zp = norm_mm(yp, g, l, 0, w_in_o, i, tm=tm_big, tn=1024)
            zs = norm_mm(ys, g, l, 0, w_in_o, i, tm=ts, tn=512)
            phi_k = (nsa_pe_k[i], nsa_w1_k[i], nsa_w2_k[i])
            phi_v = (nsa_pe_v[i], nsa_w1_v[i], nsa_w2_v[i])
            kw_d = conv_d_w.shape[1]
            kc, vc, ks, vs, kw, vw = lax.optimization_barrier(
                tuple(zp[:, off_o[j]:off_o[j] + kv_c] for j in range(1, 7)))
            rows2 = lambda a: a.reshape(1, seq * n_kv_c, HD)
            by_group = lambda c: c.reshape(-1, n_kv_c, HD).transpose(1, 0, 2).astype(BF16)
            ck = by_group(nsa_compress(rows2(kc), *phi_k, n_kv_c))
            cv = by_group(nsa_compress(rows2(vc), *phi_v, n_kv_c))
            o_c = nsa_prompt(zp, off_o[0], off_o[9], dq_c // HD, ck, cv, ks.astype(BF16),
                             vs.astype(BF16), kw.astype(BF16), vw.astype(BF16))
            o_d, hist = conformer_conv(zp, off_o[7], off_o[8], conv_d_w[i], conv_d_b[i], ln_d_g[i],
                                       ln_d_b[i], jnp.zeros((_halo_rows(kw_d, 1), d_d), F32),
                                       tm=tm_mid, shift=1)
            st['p_conv_d'].append(hist[hist.shape[0] - (kw_d - 1):][None])
            yp = mm_norm_res([o_c, o_d], w_out_o, i, g, l, 1, yp, tm=tm_mid, tn=512)
            kv4 = lambda a: a.reshape(1, -1, n_kv_c, HD)
            for n, a in zip(('p_cmp_k', 'p_cmp_v', 'p_sel_k', 'p_sel_v'), (kc, vc, ks, vs)):
                st[n].append(kv4(a))
            st['p_win_k'].append(kv4(kw[seq - wb:]))
            st['p_win_v'].append(kv4(vw[seq - wb:]))
            q, kc, vc, ks, vs, kw, vw, _, _, gl = _col_splits(_bmajor(zs, nb), split_o)
            ck = nsa_compress_paged(cache_nsa_cmp_k[i], page_table, *phi_k)
            cv = nsa_compress_paged(cache_nsa_cmp_v[i], page_table, *phi_v)
            win_k, win_v = cache_nsa_win_k[i], cache_nsa_win_v[i]
            o_c = nsa_sample(q, gl, ck, cv, win_k, win_v, ks, vs, kw, vw,
                             cache_nsa_sel_k[i], cache_nsa_sel_v[i], page_table)
            o_d, hist = conformer_conv(zs, off_o[7], off_o[8], conv_d_w[i], conv_d_b[i], ln_d_g[i],
                                       ln_d_b[i], _tmajor(state_conv_d[i]), tm=ts, shift=nb)
            st['s_conv_d'].append(_bmajor(hist, nb))
            ys = mm_norm_res([_tmajor(o_c), o_d], w_out_o, i, g, l, 1, ys, tm=ts, tn=512)
            kv4 = lambda a: a.reshape(nb, nq, n_kv_c, HD)
            for n, a in zip(('s_cmp_k', 's_cmp_v', 's_sel_k', 's_sel_v'), (kc, vc, ks, vs)):
                st[n].append(kv4(a))
            st['s_win_k'].append(jnp.concatenate([win_k, kv4(kw)], axis=1)[:, nq:])
            st['s_win_v'].append(jnp.concatenate([win_v, kv4(vw)], axis=1)[:, nq:])

        n_mem = mem.shape[0]
        mk = norm_mm(mem, mem_g, l, 0, w_mk, l, tm=n_mem, tn=512)
        mv = norm_mm(mem, mem_g, l, 0, w_mv, l, tm=n_mem, tn=512)
        st['p_mem_k'].append(mk.reshape(1, n_mem, -1, HD))
        st['p_mem_v'].append(mv.reshape(1, n_mem, -1, HD))
        yp = cross_attn(yp, g, l, 2, 3, w_mq, w_mo, l, mk, mv, tm=tm_mid)
        qs = norm_mm(ys, g, l, 2, w_mq, l, tm=ts, tn=512)
        a = mem_attn_batched(_bmajor(qs, nb), cache_mem_k, cache_mem_v, l)
        ys = mm_norm_res([_tmajor(a)], w_mo, l, g, l, 3, ys, tm=ts, tn=512)

        kw_f = ffn_conv_w.shape[1]
        gact, hist = ffn_up(yp, g, l, 4, w_up, l, ffn_conv_w[l],
                            jnp.zeros((_halo_rows(kw_f, 1), d_ff), F32), tm=tm_big, shift=1)
        st['p_ffn'].append(hist[hist.shape[0] - (kw_f - 1):, :d_ff][None])
        yp = mm_norm_res([gact], w_down, l, g, l, 5, yp, tm=tm_mid, tn=512, widths=(d_ff,))
        gact, hist = ffn_up(ys, g, l, 4, w_up, l, ffn_conv_w[l], _tmajor(state_ffn_conv[l]),
                            tm=ts, shift=nb)
        st['s_ffn'].append(_bmajor(hist[:, :d_ff], nb))
        ys = mm_norm_res([gact], w_down, l, g, l, 5, ys, tm=ts, tn=512, widths=(d_ff,))

    out = {n: jnp.stack(a) for n, a in st.items()}
    return (yp.reshape(1, seq, d), _bmajor(ys, nb)) + tuple(out[n] for n in names)
```

```python
import functools

import jax
import jax.numpy as jnp
from jax import lax
from jax.experimental import pallas as pl
from jax.experimental.pallas import tpu as pltpu

F32 = jnp.float32
BF16 = jnp.bfloat16
I32 = jnp.int32

EPS = 1e-6
NEG = -1e30
BIG = 1e4
HD = 128
PAGE = 128
CMP_BLK = 64
N_SEL = 16
WINDOW = 512
TOPK_MAX = 256
H_IDX = 16
D_IDX = 64
INT_MIN = -(2 ** 31)

VMEM_LIMIT_BYTES = 56 * 1024 * 1024


def _params(*sem):
    return pltpu.CompilerParams(dimension_semantics=sem, vmem_limit_bytes=VMEM_LIMIT_BYTES)


def _rms(x, g):
    return x * lax.rsqrt(jnp.mean(x * x, axis=-1, keepdims=True) + EPS) * g


def _dot(a, b):
    return jnp.dot(a.astype(BF16), b.astype(BF16), preferred_element_type=F32)


def _dot_nt(a, b):
    return lax.dot_general(a.astype(BF16), b.astype(BF16), (((1,), (1,)), ((), ())),
                           preferred_element_type=F32)


def _gain_spec(g, layer):
    return pl.BlockSpec((None,) + g.shape[1:], lambda *_: (layer, 0, 0))


def _norm_mm_kernel(x_ref, g_ref, w_ref, o_ref, xn_ref, *, gr):
    @pl.when(pl.program_id(1) == 0)
    def _():
        xn_ref[...] = _rms(x_ref[...], g_ref[gr:gr + 1, :]).astype(BF16)

    o_ref[...] = jnp.dot(xn_ref[...], w_ref[...].astype(BF16), preferred_element_type=F32)


def norm_mm(x, g, gl, gr, w, wl, *, tm, tn):
    m, k = x.shape
    n = w.shape[2]
    return pl.pallas_call(
        functools.partial(_norm_mm_kernel, gr=gr),
        grid=(m // tm, pl.cdiv(n, tn)),
        in_specs=[pl.BlockSpec((tm, k), lambda i, j: (i, 0)),
                  _gain_spec(g, gl),
                  pl.BlockSpec((None, k, tn), lambda i, j: (wl, 0, j))],
        out_specs=pl.BlockSpec((tm, tn), lambda i, j: (i, j)),
        out_shape=jax.ShapeDtypeStruct((m, n), F32),
        scratch_shapes=[pltpu.VMEM((tm, k), BF16)],
        compiler_params=_params("arbitrary", "arbitrary"),
        name="norm_mm",
    )(x, g, w)


def _mm_norm_res_kernel(*refs, n_in, widths, tn, nj, gr):
    a_refs = refs[:n_in]
    w_ref, g_ref, y_ref, o_ref, a_bf = refs[n_in:]
    j = pl.program_id(1)

    @pl.when(j == 0)
    def _():
        off = 0
        for a_ref, wd in zip(a_refs, widths):
            a_bf[:, off:off + wd] = a_ref[...].astype(BF16)
            off += wd

    col = pl.multiple_of(j * tn, tn)
    o_ref[:, pl.ds(col, tn)] = jnp.dot(a_bf[...], w_ref[...].astype(BF16),
                                       preferred_element_type=F32)

    @pl.when(j == nj - 1)
    def _():
        o_ref[...] = y_ref[...] + _rms(o_ref[...], g_ref[gr:gr + 1, :])


def mm_norm_res(a_list, w, wl, g, gl, gr, y, *, tm, tn, widths=None):
    m, n = y.shape
    widths = tuple(a.shape[1] for a in a_list) if widths is None else tuple(widths)
    k = sum(widths)
    assert w.shape[1:] == (k, n) and n % tn == 0
    nj = n // tn
    kern = functools.partial(_mm_norm_res_kernel, n_in=len(a_list), widths=widths, tn=tn, nj=nj,
                             gr=gr)
    return pl.pallas_call(
        kern,
        grid=(m // tm, nj),
        in_specs=[pl.BlockSpec((tm, wd), lambda i, j: (i, 0)) for wd in widths]
        + [pl.BlockSpec((None, k, tn), lambda i, j: (wl, 0, j)),
           _gain_spec(g, gl),
           pl.BlockSpec((tm, n), lambda i, j: (i, 0))],
        out_specs=pl.BlockSpec((tm, n), lambda i, j: (i, 0)),
        out_shape=jax.ShapeDtypeStruct((m, n), F32),
        scratch_shapes=[pltpu.VMEM((tm, k), BF16)],
        compiler_params=_params("arbitrary", "arbitrary"),
        name="mm_norm_res",
    )(*a_list, w, g, y)


def _order_key(x):
    b = pltpu.bitcast(x, I32)
    return b ^ ((b >> 31) & 0x7FFFFFFF)


def _kth_largest_key(count_ge, k, shape):
    t0 = jnp.where(count_ge(jnp.zeros(shape, I32)) >= k, 0, INT_MIN).astype(I32)

    def bit_body(n, t):
        cand = t | jnp.left_shift(jnp.int32(1), 30 - n)
        return jnp.where(count_ge(cand) >= k, cand, t)

    return lax.fori_loop(0, 31, bit_body, t0)


LOG2E = 1.4426950408889634
M_INIT = 0.1 * NEG


def _stage_queries(q_ref, qg_sc, n_groups, rep, tq):
    c = HD ** -0.5 * LOG2E
    for g in range(n_groups):
        for r in range(rep):
            h = g * rep + r
            qg_sc[g, r * tq:(r + 1) * tq, 0:HD] = (q_ref[:, h * HD:(h + 1) * HD] * c).astype(BF16)


def _flash_step(qg, kc, vc, bias, carry):
    m, l, acc = carry
    s = _dot_nt(qg, kc)
    if bias is not None:
        s = s + bias
    m_new = jnp.maximum(m, jnp.max(s, axis=1, keepdims=True))
    p = jnp.exp2(s - m_new)
    alpha = jnp.exp2(m - m_new)
    return (m_new, alpha * l + jnp.sum(p, axis=1, keepdims=True),
            alpha * acc + jnp.dot(p.astype(BF16), vc, preferred_element_type=F32))


def _flash_init(rows):
    return (jnp.full((rows, 1), M_INIT, F32), jnp.zeros((rows, 1), F32), jnp.zeros((rows, HD), F32))


def _flash_finish(carry):
    _, l, acc = carry
    return acc / jnp.where(l > 0, l, 1.0)


def _dsa_prompt_kernel(q_ref, qia_ref, qib_ref, iw_ref, k_ref, v_ref, ik_ref, o_ref,
                       key_sc, qs_sc, wb_sc, qg_sc, *, tq, tks, tka, topk, n_heads, n_groups):
    i = pl.program_id(0)
    q0 = i * tq
    nch_a = (q0 + tq + tka - 1) // tka
    nch_s = nch_a * (tka // tks)
    rep = n_heads // n_groups
    qpos = q0 + lax.broadcasted_iota(I32, (tq, 1), 0)

    half = H_IDX // 2
    wscale = H_IDX ** -0.5 * D_IDX ** -0.5
    for h in range(H_IDX):
        src = qia_ref if h < half else qib_ref
        hh = h % half
        qs_sc[h * tq:(h + 1) * tq, :] = src[:, hh * D_IDX:(hh + 1) * D_IDX].astype(BF16)
        wb_sc[h] = jnp.broadcast_to(iw_ref[:, D_IDX + h:D_IDX + h + 1] * wscale, (tq, 128))
    _stage_queries(q_ref, qg_sc, n_groups, rep, tq)

    def score_body(c, carry):
        col = pl.multiple_of(c * tks, tks)
        ikc = ik_ref[pl.ds(col, tks), :]
        acc = jnp.zeros((tq, tks), F32)
        for h in range(H_IDX):
            s = lax.dot_general(qs_sc[h * tq:(h + 1) * tq, :], ikc, (((1,), (1,)), ((), ())),
                                preferred_element_type=F32)
            acc = acc + jnp.concatenate([wb_sc[h]] * (tks // 128), axis=1) * jnp.maximum(s, 0.0)
        kpos = col + lax.broadcasted_iota(I32, (1, tks), 1)
        key_sc[:, pl.ds(col, tks)] = jnp.where(kpos <= qpos, _order_key(acc), INT_MIN)
        return carry

    lax.fori_loop(0, nch_s, score_body, 0)

    def count_ge(t):
        tb = jnp.broadcast_to(t, (tq, 128))

        def body(c, cnt):
            col = pl.multiple_of(c * tka, tka)
            for j in range(tka // 128):
                cnt = cnt + jnp.where(key_sc[:, pl.ds(col + j * 128, 128)] >= tb, 1, 0)
            return cnt

        cnt = lax.fori_loop(0, nch_a, body, jnp.zeros((tq, 128), I32))
        return jnp.sum(cnt, axis=1, keepdims=True)

    thr = jnp.maximum(_kth_largest_key(count_ge, topk, (tq, 1)), INT_MIN + 1)

    def bias_body(c, carry):
        col = pl.multiple_of(c * tks, tks)
        bias = jnp.where(key_sc[:, pl.ds(col, tks)] >= thr, 0.0, NEG)
        key_sc[:, pl.ds(col, tks)] = pltpu.bitcast(bias, I32)
        return carry

    lax.fori_loop(0, nch_s, bias_body, 0)

    def att_body(c, carry):
        col = pl.multiple_of(c * tka, tka)
        bias = pltpu.bitcast(key_sc[:, pl.ds(col, tka)], F32)
        bias = jnp.concatenate([bias] * rep, axis=0)
        out = []
        for g in range(n_groups):
            kc = k_ref[pl.ds(col, tka), g * HD:(g + 1) * HD]
            vc = v_ref[pl.ds(col, tka), g * HD:(g + 1) * HD]
            out.append(_flash_step(qg_sc[g], kc, vc, bias, carry[g]))
        return tuple(out)

    res = lax.fori_loop(0, nch_a, att_body, tuple(_flash_init(rep * tq) for _ in range(n_groups)))
    for g in range(n_groups):
        o = _flash_finish(res[g])
        for r in range(rep):
            o_ref[:, (g * rep + r) * HD:(g * rep + r + 1) * HD] = o[r * tq:(r + 1) * tq, :]


def dsa_prompt(z, q_off, qi_off, iw_off, n_heads, k_bf, v_bf, ik_bf, *, tq=128, tks=256, tka=512):
    t = z.shape[0]
    dq = n_heads * HD
    half = H_IDX * D_IDX // 2
    n_groups = k_bf.shape[1] // HD
    assert q_off % dq == 0 and qi_off % half == 0 and iw_off % 128 == 0 and D_IDX + H_IDX <= 128
    assert t % tka == 0 and tka % tks == 0
    topk = min(TOPK_MAX, t // 4)
    kern = functools.partial(_dsa_prompt_kernel, tq=tq, tks=tks, tka=tka, topk=topk,
                             n_heads=n_heads, n_groups=n_groups)
    full = lambda a: pl.BlockSpec(a.shape, lambda i: (0, 0))
    col = lambda w, j: pl.BlockSpec((tq, w), lambda i: (i, j))
    return pl.pallas_call(
        kern,
        grid=(t // tq,),
        in_specs=[col(dq, q_off // dq), col(half, qi_off // half), col(half, qi_off // half + 1),
                  col(128, iw_off // 128), full(k_bf), full(v_bf), full(ik_bf)],
        out_specs=pl.BlockSpec((tq, dq), lambda i: (i, 0)),
        out_shape=jax.ShapeDtypeStruct((t, dq), F32),
        scratch_shapes=[pltpu.VMEM((tq, t), I32), pltpu.VMEM((H_IDX * tq, D_IDX), BF16),
                        pltpu.VMEM((H_IDX, tq, 128), F32),
                        pltpu.VMEM((n_groups, n_heads // n_groups * tq, HD), BF16)],
        compiler_params=_params("arbitrary"),
        name="dsa_prompt",
    )(z, z, z, z, k_bf, v_bf, ik_bf)


def _nsa_compress_kernel(x_ref, pe_ref, w1_ref, w2_ref, o_ref, *, nblk, n_groups):
    _compress_rows(x_ref, pe_ref, w1_ref, w2_ref, o_ref, nblk=nblk, n_groups=n_groups)


def _compress_rows(x_ref, pe_ref, w1_ref, w2_ref, o_ref, *, nblk, n_groups):
    acc = jnp.zeros((n_groups * nblk, HD), F32)
    for p in range(CMP_BLK):
        pe_p = pe_ref[p:p + 1, :]
        lhs = jnp.concatenate(
            [x_ref[pl.ds(p * n_groups + g, nblk, stride=CMP_BLK * n_groups), :] + pe_p
             for g in range(n_groups)], axis=0)
        acc = acc + _dot(lhs, w1_ref[p * HD:(p + 1) * HD, :])
    out = _dot(jnp.maximum(acc, 0.0), w2_ref[...])
    for g in range(n_groups):
        o_ref[:, g * HD:(g + 1) * HD] = out[g * nblk:(g + 1) * nblk, :]


def nsa_compress(x, pe, w1, w2, n_groups):
    b, tg, _ = x.shape
    nblk = tg // (n_groups * CMP_BLK)
    kern = functools.partial(_nsa_compress_kernel, nblk=nblk, n_groups=n_groups)
    full = lambda a: pl.BlockSpec(a.shape, lambda i: (0,) * a.ndim)
    return pl.pallas_call(
        kern,
        grid=(b,),
        in_specs=[pl.BlockSpec((None, tg, HD), lambda i: (i, 0, 0)), full(pe), full(w1), full(w2)],
        out_specs=pl.BlockSpec((None, nblk, n_groups * HD), lambda i: (i, 0, 0)),
        out_shape=jax.ShapeDtypeStruct((b, nblk, n_groups * HD), F32),
        compiler_params=_params("arbitrary"),
        name="nsa_compress",
    )(x, pe, w1, w2)


def _masked_softmax(s, ok):
    s = jnp.where(ok, s, NEG)
    e = jnp.where(ok, jnp.exp(s - jnp.max(s, axis=-1, keepdims=True)), 0.0)
    d = jnp.sum(e, axis=-1, keepdims=True)
    return e / jnp.where(d > 0, d, 1.0)


def _top_blocks(score, n_sel, axis=1):
    nb = score.shape[axis]
    pos = lax.broadcasted_iota(I32, score.shape, axis)
    sel = jnp.zeros(score.shape, F32)
    for _ in range(n_sel):
        m = jnp.max(score, axis=axis, keepdims=True)
        idx = jnp.min(jnp.where(score == m, pos, nb), axis=axis, keepdims=True)
        hit = pos == idx
        sel = jnp.where(hit, 1.0, sel)
        score = jnp.where(hit, -3e38, score)
    return sel


def _masked_softmax2(s, ok):
    s = jnp.where(ok, s, NEG)
    e = jnp.where(ok, jnp.exp2(s - jnp.max(s, axis=-1, keepdims=True)), 0.0)
    d = jnp.sum(e, axis=-1, keepdims=True)
    return e / jnp.where(d > 0, d, 1.0)


def _nsa_prompt_kernel(q_ref, gl_ref, ck_ref, cv_ref, ks_ref, vs_ref, kw_ref, vw_ref, o_ref,
                       qg_sc, *, tq, tk, n_heads, n_groups, nbc, gl_lane):
    i = pl.program_id(0)
    q0 = i * tq
    rep = n_heads // n_groups
    qpos = q0 + lax.broadcasted_iota(I32, (tq, 1), 0)
    jb = lax.broadcasted_iota(I32, (1, nbc), 1)
    cmp_ok = jnp.concatenate([jnp.where(((jb + 1) * CMP_BLK - 1) <= qpos, 1, 0)] * rep, axis=0) > 0
    cur = qpos // CMP_BLK
    gates = jax.nn.sigmoid(gl_ref[:, gl_lane:gl_lane + 3 * n_heads])

    _stage_queries(q_ref, qg_sc, n_groups, rep, tq)
    w0 = pl.multiple_of(jnp.maximum(q0 - WINDOW, 0), tq)
    wlen = WINDOW + tq
    diff = qpos - (w0 + lax.broadcasted_iota(I32, (1, wlen), 1))
    w_ok = jnp.where((diff >= 0) & (diff <= WINDOW), 1, 0)
    w_ok = jnp.concatenate([w_ok] * rep, axis=0) > 0
    imps = []
    for g in range(n_groups):
        qg = qg_sc[g, :, 0:HD]
        p_c = _masked_softmax2(_dot_nt(qg, ck_ref[g]), cmp_ok)
        o_c = _dot(p_c, cv_ref[g])
        imp = p_c[0:tq]
        for r in range(1, rep):
            imp = imp + p_c[r * tq:(r + 1) * tq]
        imps.append(imp)
        kw = kw_ref[pl.ds(w0, wlen), g * HD:(g + 1) * HD]
        vw = vw_ref[pl.ds(w0, wlen), g * HD:(g + 1) * HD]
        o_w = _dot(_masked_softmax2(_dot_nt(qg, kw), w_ok), vw)
        for r in range(rep):
            h = g * rep + r
            rows = slice(r * tq, (r + 1) * tq)
            o_ref[:, h * HD:(h + 1) * HD] = (gates[:, 3 * h:3 * h + 1] * o_c[rows]
                                             + gates[:, 3 * h + 2:3 * h + 3] * o_w[rows])

    imp_t = jnp.concatenate([imp.T for imp in imps], axis=1)
    jb_t = lax.broadcasted_iota(I32, (nbc, 1), 0)
    cur_t = (q0 + lax.broadcasted_iota(I32, (1, tq), 1)) // CMP_BLK
    cur_t = jnp.concatenate([cur_t] * n_groups, axis=1)
    blk_score = jnp.where(jb_t > cur_t, NEG, jnp.where((jb_t == cur_t) | (jb_t == 0), BIG, imp_t))
    sel_t = _top_blocks(blk_score, min(N_SEL, nbc), axis=0)
    for g in range(n_groups):
        pen = ((sel_t[:, g * tq:(g + 1) * tq].T - 1.0) * (-NEG)).astype(BF16)
        for r in range(rep):
            qg_sc[g, r * tq:(r + 1) * tq, HD:HD + nbc] = pen

    nch = (q0 + tq + tk - 1) // tk
    kw_aug = HD + nbc

    def chunk(c, carry, bias):
        col = pl.multiple_of(c * tk, tk)
        out = []
        for g in range(n_groups):
            kc = ks_ref[pl.ds(col, tk), g * kw_aug:(g + 1) * kw_aug]
            vc = vs_ref[pl.ds(col, tk), g * HD:(g + 1) * HD]
            out.append(_flash_step(qg_sc[g], kc, vc, bias, carry[g]))
        return tuple(out)

    init = tuple(_flash_init(rep * tq) for _ in range(n_groups))
    res = lax.fori_loop(0, nch - 1, lambda c, carry: chunk(c, carry, None), init)
    kpos = (nch - 1) * tk + lax.broadcasted_iota(I32, (1, tk), 1)
    causal = jnp.concatenate([jnp.where(kpos <= qpos, 0.0, NEG)] * rep, axis=0)
    res = chunk(nch - 1, res, causal)

    for g in range(n_groups):
        o_s = _flash_finish(res[g])
        for r in range(rep):
            h = g * rep + r
            o_ref[:, h * HD:(h + 1) * HD] += gates[:, 3 * h + 1:3 * h + 2] * o_s[r * tq:(r + 1) * tq]


def nsa_prompt(z, q_off, gl_off, n_heads, ck_bf, cv_bf, ks_bf, vs_bf, kw_bf, vw_bf, *, tq=128, tk=512):
    t = z.shape[0]
    dq = n_heads * HD
    n_groups, nbc, _ = ck_bf.shape
    gl_lane = gl_off % 128
    assert t >= WINDOW + tq and t % tk == 0 and nbc == t // CMP_BLK
    assert q_off % dq == 0 and gl_lane + 3 * n_heads <= 128
    kern = functools.partial(_nsa_prompt_kernel, tq=tq, tk=tk, n_heads=n_heads,
                             n_groups=n_groups, nbc=nbc, gl_lane=gl_lane)
    onehot = (jnp.arange(t, dtype=I32)[:, None] // CMP_BLK
              == jnp.arange(nbc, dtype=I32)[None, :]).astype(BF16)
    ks_aug = jnp.concatenate(
        [a for g in range(n_groups) for a in (ks_bf[:, g * HD:(g + 1) * HD], onehot)], axis=1)
    full = lambda a: pl.BlockSpec(a.shape, lambda i: (0,) * a.ndim)
    col = lambda w, j: pl.BlockSpec((tq, w), lambda i: (i, j))
    return pl.pallas_call(
        kern,
        grid=(t // tq,),
        in_specs=[col(dq, q_off // dq), col(128, gl_off // 128), full(ck_bf), full(cv_bf),
                  full(ks_aug), full(vs_bf), full(kw_bf), full(vw_bf)],
        out_specs=pl.BlockSpec((tq, dq), lambda i: (i, 0)),
        out_shape=jax.ShapeDtypeStruct((t, dq), F32),
        scratch_shapes=[pltpu.VMEM((n_groups, n_heads // n_groups * tq, HD + nbc), BF16)],
        compiler_params=_params("arbitrary"),
        name="nsa_prompt",
    )(z, z, ck_bf, cv_bf, ks_aug, vs_bf, kw_bf, vw_bf)


def _halo_rows(kw, shift):
    return max(8, (kw - 1) * shift)


SUBLANES = 8


def _dwconv(ext_ref, w_ref, u, prev_ref, first, *, tm, kw, shift, part_ref=None):
    hp = _halo_rows(kw, shift)
    base = hp - (kw - 1) * shift

    @pl.when(first)
    def _():
        ext_ref[0:hp, :] = prev_ref[...]
        if part_ref is not None:
            ext_ref[hp + tm:hp + tm + SUBLANES, :] = jnp.zeros((SUBLANES, ext_ref.shape[1]), F32)

    ext_ref[hp:hp + tm, :] = u
    y = None
    if part_ref is None:
        for i in range(kw):
            term = w_ref[i:i + 1, :] * ext_ref[pl.ds(base + i * shift, tm), :]
            y = term if y is None else y + term
    else:
        assert shift == 1 and base % SUBLANES == 0
        for b in range(min(SUBLANES, kw)):
            part = None
            for i in range(b, kw, SUBLANES):
                term = w_ref[i:i + 1, :] * ext_ref[pl.ds(base + i - b, tm + SUBLANES), :]
                part = term if part is None else part + term
            if b == 0:
                y = part[0:tm]
            else:
                part_ref[...] = part
                y = y + part_ref[pl.ds(b, tm), :]
    tail = ext_ref[tm:tm + hp, :]
    ext_ref[0:hp, :] = tail
    return y, tail


def _gated_conv_kernel(xa_ref, bg_ref, cg_ref, w_ref, prev_ref, o_ref, st_ref, ext_ref,
                       *, tm, kw, shift):
    u = cg_ref[...] * xa_ref[...]
    y, tail = _dwconv(ext_ref, w_ref, u, prev_ref, pl.program_id(0) == 0, tm=tm, kw=kw, shift=shift)
    o_ref[...] = bg_ref[...] * y
    st_ref[...] = tail


def gated_conv(z, c, w, prev, *, tm, shift):
    m = z.shape[0]
    kw = w.shape[0]
    hp = _halo_rows(kw, shift)
    kern = functools.partial(_gated_conv_kernel, tm=tm, kw=kw, shift=shift)
    col = lambda j: pl.BlockSpec((tm, c), lambda i: (i, j))
    return pl.pallas_call(
        kern,
        grid=(m // tm,),
        in_specs=[col(0), col(1), col(2), pl.BlockSpec((kw, c), lambda i: (0, 0)),
                  pl.BlockSpec((hp, c), lambda i: (0, 0))],
        out_specs=[pl.BlockSpec((tm, c), lambda i: (i, 0)), pl.BlockSpec((hp, c), lambda i: (0, 0))],
        out_shape=[jax.ShapeDtypeStruct((m, c), F32), jax.ShapeDtypeStruct((hp, c), F32)],
        scratch_shapes=[pltpu.VMEM((hp + tm, c), F32)],
        compiler_params=_params("arbitrary"),
        name="gated_conv",
    )(z, z, z, w, prev)


def _conformer_kernel(dpa_ref, dpb_ref, dga_ref, dgb_ref, w_ref, b_ref, lg_ref, lb_ref, prev_ref,
                      o_ref, st_ref, ext_ref, *part_ref, tm, kw, shift):
    u = jnp.concatenate([dpa_ref[...] * jax.nn.sigmoid(dga_ref[...]),
                         dpb_ref[...] * jax.nn.sigmoid(dgb_ref[...])], axis=1)
    c, tail = _dwconv(ext_ref, w_ref, u, prev_ref, pl.program_id(0) == 0, tm=tm, kw=kw, shift=shift,
                      part_ref=part_ref[0] if part_ref else None)
    c = c + b_ref[...]
    mu = jnp.mean(c, axis=-1, keepdims=True)
    xc = c - mu
    y = xc * lax.rsqrt(jnp.mean(xc * xc, axis=-1, keepdims=True) + EPS) * lg_ref[...] + lb_ref[...]
    o_ref[...] = y * jax.nn.sigmoid(y)
    st_ref[...] = tail


def conformer_conv(z, dp_off, dg_off, w, b, ln_g, ln_b, prev, *, tm, shift):
    m = z.shape[0]
    kw, c = w.shape
    half = c // 2
    assert dp_off % half == 0 and dg_off % half == 0
    hp = _halo_rows(kw, shift)
    kern = functools.partial(_conformer_kernel, tm=tm, kw=kw, shift=shift)
    col = lambda off, j: pl.BlockSpec((tm, half), lambda i: (i, off // half + j))
    row = pl.BlockSpec((tm, c), lambda i: (i, 0))
    vec = pl.BlockSpec((1, c), lambda i: (0, 0))
    grouped = shift == 1 and (hp - (kw - 1)) % SUBLANES == 0
    scratch = ([pltpu.VMEM((hp + tm + SUBLANES, c), F32), pltpu.VMEM((tm + SUBLANES, c), F32)]
               if grouped else [pltpu.VMEM((hp + tm, c), F32)])
    return pl.pallas_call(
        kern,
        grid=(m // tm,),
        in_specs=[col(dp_off, 0), col(dp_off, 1), col(dg_off, 0), col(dg_off, 1),
                  pl.BlockSpec((kw, c), lambda i: (0, 0)), vec, vec, vec,
                  pl.BlockSpec((hp, c), lambda i: (0, 0))],
        out_specs=[row, pl.BlockSpec((hp, c), lambda i: (0, 0))],
        out_shape=[jax.ShapeDtypeStruct((m, c), F32), jax.ShapeDtypeStruct((hp, c), F32)],
        scratch_shapes=scratch,
        compiler_params=_params("arbitrary"),
        name="conformer_conv",
    )(z, z, z, z, w, b.reshape(1, c), ln_g.reshape(1, c), ln_b.reshape(1, c), prev)


FFN_TF = 128
FFN_NB = 4


def _ffn_up_kernel(*refs, tm, kw, shift, gr):
    y_ref, g_ref = refs[:2]
    wa_refs = refs[2:2 + FFN_NB]
    wv_refs = refs[2 + FFN_NB:2 + 2 * FFN_NB]
    cw_ref, prev_ref, o_ref, st_ref, xn_ref, wcat_ref, halo_ref, ext_ref = refs[2 + 2 * FFN_NB:]
    i = pl.program_id(0)
    j = pl.program_id(1)
    hp = _halo_rows(kw, shift)
    tf = FFN_NB * FFN_TF

    @pl.when(j == 0)
    def _():
        xn_ref[...] = _rms(y_ref[...], g_ref[gr:gr + 1, :]).astype(BF16)

    @pl.when(i == 0)
    def _():
        ext_ref[0:hp, :] = prev_ref[...]

    @pl.when(i > 0)
    def _():
        ext_ref[0:hp, :] = halo_ref[j]

    def granule_matmul(s):
        wcat_ref[s, :, 0:FFN_TF] = wa_refs[s][...].astype(BF16)
        wcat_ref[s, :, FFN_TF:2 * FFN_TF] = wv_refs[s][...].astype(BF16)
        return jnp.dot(xn_ref[...], wcat_ref[s], preferred_element_type=F32)

    h_next = granule_matmul(0)
    for s in range(FFN_NB):
        cols = slice(s * FFN_TF, (s + 1) * FFN_TF)
        h = h_next
        if s + 1 < FFN_NB:
            h_next = granule_matmul(s + 1)
        ext_ref[hp:hp + tm, cols] = h[:, 0:FFN_TF]
        c = None
        for t in range(kw):
            term = cw_ref[t:t + 1, cols] * ext_ref[pl.ds(hp - (kw - 1 - t) * shift, tm), cols]
            c = term if c is None else c + term
        o_ref[:, cols] = (c * jax.nn.sigmoid(c) * h[:, FFN_TF:2 * FFN_TF]).astype(BF16)
    tail = ext_ref[tm:tm + hp, :]
    halo_ref[j] = tail
    st_ref[...] = tail


def ffn_up(y, g, gl, gr, w_up, wl, conv_w, prev, *, tm, shift):
    m, d = y.shape
    kw, f = conv_w.shape
    nf = f // FFN_TF
    assert f % FFN_TF == 0 and w_up.shape[1:] == (d, 2 * f)
    tf = FFN_NB * FFN_TF
    nj = pl.cdiv(nf, FFN_NB)
    fp = nj * tf
    hp = _halo_rows(kw, shift)
    nm = m // tm
    conv_w = jnp.pad(conv_w, ((0, 0), (0, fp - f)))
    prev = jnp.pad(prev, ((0, 0), (0, fp - f)))
    kern = functools.partial(_ffn_up_kernel, tm=tm, kw=kw, shift=shift, gr=gr)

    def granule(base, s):
        return pl.BlockSpec((None, d, FFN_TF),
                            lambda i, j: (wl, 0, base + jnp.minimum(FFN_NB * j + s, nf - 1)))

    return pl.pallas_call(
        kern,
        grid=(nm, nj),
        in_specs=[pl.BlockSpec((tm, d), lambda i, j: (i, 0)), _gain_spec(g, gl)]
        + [granule(0, s) for s in range(FFN_NB)] + [granule(nf, s) for s in range(FFN_NB)]
        + [pl.BlockSpec((kw, tf), lambda i, j: (0, j)), pl.BlockSpec((hp, tf), lambda i, j: (0, j))],
        out_specs=[pl.BlockSpec((tm, tf), lambda i, j: (i, j)),
                   pl.BlockSpec((hp, tf), lambda i, j: (i, j))],
        out_shape=[jax.ShapeDtypeStruct((m, fp), BF16), jax.ShapeDtypeStruct((nm * hp, fp), F32)],
        scratch_shapes=[pltpu.VMEM((tm, d), BF16), pltpu.VMEM((FFN_NB, d, 2 * FFN_TF), BF16),
                        pltpu.VMEM((nj, hp, tf), F32), pltpu.VMEM((hp + tm, tf), F32)],
        compiler_params=_params("arbitrary", "arbitrary"),
        name="ffn_up",
    )(y, g, *([w_up] * (2 * FFN_NB)), conv_w, prev)


def _mem_heads(q, mk_head, mv_head, n_heads):
    scale = HD ** -0.5
    outs = []
    for h in range(n_heads):
        s = _dot_nt(q[:, h * HD:(h + 1) * HD], mk_head(h)) * scale
        e = jnp.exp(s - jnp.max(s, axis=-1, keepdims=True))
        p = e / jnp.sum(e, axis=-1, keepdims=True)
        outs.append(_dot(p, mv_head(h)))
    return jnp.concatenate(outs, axis=1)


def _xattn_kernel(y_ref, g_ref, wq_ref, mk_ref, mv_ref, wo_ref, o_ref, wq_bf, wo_bf,
                  *, n_heads, gr_in, gr_out):
    @pl.when(pl.program_id(0) == 0)
    def _():
        wq_bf[...] = wq_ref[...].astype(BF16)
        wo_bf[...] = wo_ref[...].astype(BF16)

    y = y_ref[...]
    q = jnp.dot(_rms(y, g_ref[gr_in:gr_in + 1, :]).astype(BF16), wq_bf[...],
                preferred_element_type=F32)
    o = _mem_heads(q, lambda h: mk_ref[:, h * HD:(h + 1) * HD],
                   lambda h: mv_ref[:, h * HD:(h + 1) * HD], n_heads)
    f = jnp.dot(o.astype(BF16), wo_bf[...], preferred_element_type=F32)
    o_ref[...] = y + _rms(f, g_ref[gr_out:gr_out + 1, :])


def cross_attn(y, g, gl, gr_in, gr_out, wq, wo, wl, mk, mv, *, tm):
    m, d = y.shape
    dh = wq.shape[2]
    kern = functools.partial(_xattn_kernel, n_heads=dh // HD, gr_in=gr_in, gr_out=gr_out)
    full = lambda a: pl.BlockSpec(a.shape, lambda i: (0, 0))
    layer = lambda a: pl.BlockSpec((None,) + a.shape[1:], lambda i: (wl, 0, 0))
    row = pl.BlockSpec((tm, d), lambda i: (i, 0))
    return pl.pallas_call(
        kern,
        grid=(m // tm,),
        in_specs=[row, _gain_spec(g, gl), layer(wq), full(mk), full(mv), layer(wo)],
        out_specs=row,
        out_shape=jax.ShapeDtypeStruct((m, d), F32),
        scratch_shapes=[pltpu.VMEM(wq.shape[1:], BF16), pltpu.VMEM(wo.shape[1:], BF16)],
        compiler_params=_params("arbitrary"),
        name="cross_attn",
    )(y, g, wq, mk, mv, wo)


def _mem_attn_kernel(q_ref, mk_ref, mv_ref, o_ref, *, n_heads, bb):
    for j in range(bb):
        o_ref[j] = _mem_heads(q_ref[j], lambda h: mk_ref[j, :, h, :], lambda h: mv_ref[j, :, h, :],
                              n_heads)


def mem_attn_batched(q, mk, mv, layer, *, bb=4):
    b, t, dh = q.shape
    bb = bb if b % bb == 0 else 1
    kern = functools.partial(_mem_attn_kernel, n_heads=dh // HD, bb=bb)
    cache = pl.BlockSpec((None, bb) + mk.shape[2:], lambda i: (layer, i, 0, 0, 0))
    blk = pl.BlockSpec((bb, t, dh), lambda i: (i, 0, 0))
    return pl.pallas_call(
        kern,
        grid=(b // bb,),
        in_specs=[blk, cache, cache],
        out_specs=blk,
        out_shape=jax.ShapeDtypeStruct(q.shape, F32),
        compiler_params=_params("arbitrary"),
        name="mem_attn_batched",
    )(q, mk, mv)


def _page_copy(cache, pt_ref, b, p, buf, slot, sem):
    rows, lanes = cache.shape[1:]
    if buf.shape[2] == lanes:
        dst = buf.at[slot, pl.ds(p * rows, rows), :]
    else:
        dst = buf.at[slot, :, pl.ds(pl.multiple_of(p * lanes, lanes), lanes)]
    return pltpu.make_async_copy(cache.at[pt_ref[b, p]], dst, sem.at[slot])


def _for_pages(caches, bufs, sems, pt_ref, b, slot, n_pages, fn):
    def body(p, carry):
        for cache, buf, sem in zip(caches, bufs, sems):
            fn(_page_copy(cache, pt_ref, b, p, buf, slot, sem))
        return carry

    lax.fori_loop(0, n_pages, body, 0)


def _fetch_pages(caches, bufs, sems, pt_ref, n_pages):
    b = pl.program_id(0)
    slot = b % 2
    args = (caches, bufs, sems, pt_ref)

    @pl.when(b == 0)
    def _():
        _for_pages(*args, b, slot, n_pages, lambda cp: cp.start())

    @pl.when(b + 1 < pl.num_programs(0))
    def _():
        _for_pages(*args, b + 1, 1 - slot, n_pages, lambda cp: cp.start())

    _for_pages(*args, b, slot, n_pages, lambda cp: cp.wait())
    return slot


def _paged_call(kern, pt, inputs, n_any, out_shape, out_block, scratch, name):
    nb = pt.shape[0]

    def spec(a):
        nd = a.ndim - 1
        return pl.BlockSpec((None,) + a.shape[1:], lambda i, pt_ref: (i,) + (0,) * nd)

    def shared(a):
        nd = a.ndim
        return pl.BlockSpec(a.shape, lambda i, pt_ref: (0,) * nd)

    in_specs = []
    for a in inputs[:len(inputs) - n_any]:
        in_specs.append(spec(a) if a.shape[0] == nb and a.ndim >= 3 else shared(a))
    in_specs += [pl.BlockSpec(memory_space=pl.ANY)] * n_any
    nd_o = len(out_block)
    return pl.pallas_call(
        kern,
        grid_spec=pltpu.PrefetchScalarGridSpec(
            num_scalar_prefetch=1,
            grid=(nb,),
            in_specs=in_specs,
            out_specs=pl.BlockSpec((None,) + tuple(out_block), lambda i, pt_ref: (i,) + (0,) * nd_o),
            scratch_shapes=scratch),
        out_shape=jax.ShapeDtypeStruct((nb,) + tuple(out_block), out_shape),
        compiler_params=_params("arbitrary"),
        name=name,
    )(pt, *inputs)


def _attend_two_parts(qg, k_a, v_a, ok_a, k_b, v_b, ok_b):
    scale = HD ** -0.5
    s_a = jnp.where(ok_a, _dot_nt(qg, k_a) * scale, NEG)
    s_b = jnp.where(ok_b, _dot_nt(qg, k_b) * scale, NEG)
    mx = jnp.maximum(jnp.max(s_a, axis=1, keepdims=True), jnp.max(s_b, axis=1, keepdims=True))
    e_a = jnp.where(ok_a, jnp.exp(s_a - mx), 0.0)
    e_b = jnp.where(ok_b, jnp.exp(s_b - mx), 0.0)
    den = jnp.sum(e_a, axis=1, keepdims=True) + jnp.sum(e_b, axis=1, keepdims=True)
    return (_dot(e_a, v_a) + _dot(e_b, v_b)) / jnp.where(den > 0, den, 1.0)


def _dsa_sample_select_kernel(pt_ref, qi_ref, wi_ref, ikn_ref, ik_hbm, m_ref, ikbuf, key_sc, sem,
                              *, n_pages, tk, topk, n_new, nq, rows):
    slot = _fetch_pages([ik_hbm], [ikbuf], [sem], pt_ref, n_pages)
    past = n_pages * PAGE
    qi = qi_ref[...].astype(BF16)
    w = wi_ref[...] * (H_IDX ** -0.5 * D_IDX ** -0.5)

    def scores(ikc_t):
        r = jnp.maximum(_dot(qi, ikc_t), 0.0) * w
        return jnp.sum(r.reshape(H_IDX, rows, ikc_t.shape[1]), axis=0)

    def score_body(c, carry):
        col = pl.multiple_of(c * tk, tk)
        key_sc[:, pl.ds(col, tk)] = _order_key(scores(ikbuf[slot, :, pl.ds(col, tk)]))
        return carry

    lax.fori_loop(0, past // tk, score_body, 0)
    t_row = lax.rem(lax.broadcasted_iota(I32, (rows, PAGE), 0), nq)
    col = lax.broadcasted_iota(I32, (rows, PAGE), 1)
    new_ok = (col < n_new) & (col <= t_row)
    key_sc[:, past:past + PAGE] = jnp.where(new_ok, _order_key(scores(ikn_ref[...])), INT_MIN)

    def count_ge(t):
        return jnp.sum(jnp.where(key_sc[...] >= t, 1, 0), axis=1, keepdims=True)

    thr = jnp.maximum(_kth_largest_key(count_ge, topk, (rows, 1)), INT_MIN + 1)
    m_ref[...] = jnp.where(key_sc[...] >= thr, 1.0, 0.0)


def _dsa_sample_attn_kernel(pt_ref, q_ref, m_ref, kn_ref, vn_ref, k_hbm, v_hbm, o_ref,
                            kbuf, vbuf, ksem, vsem, *, n_pages, n_groups):
    slot = _fetch_pages([k_hbm, v_hbm], [kbuf, vbuf], [ksem, vsem], pt_ref, n_pages)
    past = n_pages * PAGE
    tile = q_ref.shape[1] // m_ref.shape[0]
    ok_c = jnp.concatenate([m_ref[:, 0:past]] * tile, axis=0) > 0.5
    ok_n = jnp.concatenate([m_ref[:, past:past + PAGE]] * tile, axis=0) > 0.5
    for g in range(n_groups):
        o_ref[g] = _attend_two_parts(
            q_ref[g].astype(BF16),
            kbuf[slot, pl.ds(g, past, stride=n_groups), :],
            vbuf[slot, pl.ds(g, past, stride=n_groups), :], ok_c,
            kn_ref[pl.ds(g, PAGE, stride=n_groups), :],
            vn_ref[pl.ds(g, PAGE, stride=n_groups), :], ok_n)


def _group_rows(x, n_groups):
    b, t, dq = x.shape
    rep = dq // HD // n_groups
    return x.reshape(b, t, n_groups, rep, HD).transpose(0, 2, 3, 1, 4).reshape(b, n_groups, rep * t, HD)


def _ungroup_rows(o, t):
    b, g, rows, _ = o.shape
    rep = rows // t
    return o.reshape(b, g, rep, t, HD).transpose(0, 3, 1, 2, 4).reshape(b, t, g * rep * HD)


def _pad_new_rows(x, n_groups):
    b, t, _ = x.shape
    x = x.reshape(b, t * n_groups, HD)
    return jnp.pad(x, ((0, 0), (0, (PAGE - t) * n_groups), (0, 0)))


def dsa_sample(q, qi, wi, k_new, v_new, ik_new, cache_k, cache_v, cache_ik, pt, *, tk=1024):
    nb, nq, dq = q.shape
    n_pages = pt.shape[1]
    n_groups = cache_k.shape[2]
    rep = dq // HD // n_groups
    rows = rep * nq
    past = n_pages * PAGE
    topk = min(TOPK_MAX, (past + nq) // 4)
    width = past + PAGE
    rep_s = max(1, 8 // nq)
    rows_s = rep_s * nq
    assert rows % rows_s == 0
    qi_r = jnp.broadcast_to(qi.reshape(nb, nq, H_IDX, D_IDX).transpose(0, 2, 1, 3)[:, :, None],
                            (nb, H_IDX, rep_s, nq, D_IDX)).reshape(nb, H_IDX * rows_s, D_IDX)
    wi_r = jnp.broadcast_to(wi.transpose(0, 2, 1)[:, :, None],
                            (nb, H_IDX, rep_s, nq)).reshape(nb, H_IDX * rows_s, 1)
    ikn = jnp.swapaxes(jnp.pad(ik_new, ((0, 0), (0, PAGE - nq), (0, 0))), 1, 2)
    cache_ik = jnp.swapaxes(cache_ik, 1, 2)
    sel_kern = functools.partial(_dsa_sample_select_kernel, n_pages=n_pages, tk=tk, topk=topk,
                                 n_new=nq, nq=nq, rows=rows_s)
    mask = _paged_call(
        sel_kern, pt, [qi_r, wi_r, ikn, cache_ik], 1, F32, (rows_s, width),
        [pltpu.VMEM((2, D_IDX, past), F32), pltpu.VMEM((rows_s, width), I32),
         pltpu.SemaphoreType.DMA((2,))], "dsa_sample_select")
    ck = cache_k.reshape(cache_k.shape[0], PAGE * n_groups, HD)
    cv = cache_v.reshape(cache_v.shape[0], PAGE * n_groups, HD)
    att_kern = functools.partial(_dsa_sample_attn_kernel, n_pages=n_pages, n_groups=n_groups)
    o = _paged_call(
        att_kern, pt, [_group_rows(q, n_groups), mask, _pad_new_rows(k_new, n_groups),
                       _pad_new_rows(v_new, n_groups), ck, cv], 2, F32, (n_groups, rows, HD),
        [pltpu.VMEM((2, past * n_groups, HD), F32), pltpu.VMEM((2, past * n_groups, HD), F32),
         pltpu.SemaphoreType.DMA((2,)), pltpu.SemaphoreType.DMA((2,))], "dsa_sample_attn")
    return _ungroup_rows(o, nq)


def _nsa_compress_paged_kernel(pt_ref, pe_ref, w1_ref, w2_ref, x_hbm, o_ref, buf, sem,
                               *, n_pages, n_groups):
    slot = _fetch_pages([x_hbm], [buf], [sem], pt_ref, n_pages)
    _compress_rows(buf.at[slot], pe_ref, w1_ref, w2_ref, o_ref,
                   nblk=n_pages * PAGE // CMP_BLK, n_groups=n_groups)


def nsa_compress_paged(cache, pt, pe, w1, w2):
    n_groups = cache.shape[2]
    n_pages = pt.shape[1]
    past = n_pages * PAGE
    kern = functools.partial(_nsa_compress_paged_kernel, n_pages=n_pages, n_groups=n_groups)
    view = cache.reshape(cache.shape[0], PAGE * n_groups, HD)
    return _paged_call(
        kern, pt, [pe, w1.astype(BF16), w2, view], 1, F32, (past // CMP_BLK, n_groups * HD),
        [pltpu.VMEM((2, past * n_groups, HD), F32), pltpu.SemaphoreType.DMA((2,))],
        "nsa_compress_paged")


def _nsa_sample_kernel(pt_ref, q_ref, gl_ref, ck_ref, cv_ref, ex_ref, wk_ref, wv_ref, skn_ref, svn_ref,
                       wkn_ref, wvn_ref, ks_hbm, vs_hbm, o_ref, kbuf, vbuf, ksem, vsem,
                       *, n_pages, n_groups, nq, n_new):
    slot = _fetch_pages([ks_hbm, vs_hbm], [kbuf, vbuf], [ksem, vsem], pt_ref, n_pages)
    past = n_pages * PAGE
    scale = HD ** -0.5
    rows = q_ref.shape[1]
    rep = rows // nq
    nbc = ck_ref.shape[0]
    wlen = wk_ref.shape[0] // n_groups
    t_row = lax.rem(lax.broadcasted_iota(I32, (rows, 1), 0), nq)
    qpos = past + t_row
    jb = lax.broadcasted_iota(I32, (1, nbc), 1)
    cmp_ok = ((jb + 1) * CMP_BLK - 1) <= qpos
    cur = qpos // CMP_BLK
    newcol = lax.broadcasted_iota(I32, (1, PAGE), 1)
    new_ok = (newcol < n_new) & (newcol <= t_row)
    diff = qpos - (past - wlen + lax.broadcasted_iota(I32, (1, wlen), 1))
    win_ok = (diff >= 0) & (diff <= WINDOW)
    o_cs, scores = [], []
    for g in range(n_groups):
        cols = slice(g * HD, (g + 1) * HD)
        p_c = _masked_softmax(_dot_nt(q_ref[g].astype(BF16), ck_ref[:, cols]) * scale, cmp_ok)
        o_cs.append(_dot(p_c, cv_ref[:, cols]))
        imp = p_c
        for r in range(1, rep):
            imp = imp + pltpu.roll(p_c, r * nq, 0)
        scores.append(jnp.where(jb > cur, NEG, jnp.where((jb == cur) | (jb == 0), BIG, imp)))
    selm = _top_blocks(jnp.concatenate(scores, axis=0).T, N_SEL - 1, axis=0).T.astype(BF16)
    picked_all = jnp.dot(selm, ex_ref[...], preferred_element_type=F32)
    for g in range(n_groups):
        qg = q_ref[g].astype(BF16)
        gates = jax.nn.sigmoid(gl_ref[g])
        o_c = o_cs[g]
        picked = picked_all[g * rows:(g + 1) * rows] > 0.5
        o_s = _attend_two_parts(
            qg, kbuf[slot, pl.ds(g, past, stride=n_groups), :],
            vbuf[slot, pl.ds(g, past, stride=n_groups), :], picked,
            skn_ref[pl.ds(g, PAGE, stride=n_groups), :],
            svn_ref[pl.ds(g, PAGE, stride=n_groups), :], new_ok)
        o_w = _attend_two_parts(
            qg, wk_ref[pl.ds(g, wlen, stride=n_groups), :],
            wv_ref[pl.ds(g, wlen, stride=n_groups), :], win_ok,
            wkn_ref[pl.ds(g, PAGE, stride=n_groups), :],
            wvn_ref[pl.ds(g, PAGE, stride=n_groups), :], new_ok)
        o_ref[g] = gates[:, 0:1] * o_c + gates[:, 1:2] * o_s + gates[:, 2:3] * o_w


def nsa_sample(q, gl, ck, cv, win_k, win_v, ks_new, vs_new, kw_new, vw_new, cache_ks, cache_vs, pt):
    nb, nq, dq = q.shape
    n_pages = pt.shape[1]
    n_groups = cache_ks.shape[2]
    n_heads = dq // HD
    rep = n_heads // n_groups
    rows = rep * nq
    past = n_pages * PAGE
    assert past % CMP_BLK == 0 and nq <= CMP_BLK and ck.shape[1] == past // CMP_BLK
    assert win_k.shape[1] == WINDOW
    glr = gl.reshape(nb, nq, n_groups, rep, 3).transpose(0, 2, 3, 1, 4).reshape(nb, n_groups, rows, 3)
    view = lambda c: c.reshape(c.shape[0], c.shape[1] * n_groups, HD)
    kern = functools.partial(_nsa_sample_kernel, n_pages=n_pages, n_groups=n_groups,
                             nq=nq, n_new=nq)
    pad = lambda x: _pad_new_rows(x, n_groups)
    expand = (jnp.arange(past, dtype=I32)[None, :] // CMP_BLK
              == jnp.arange(past // CMP_BLK, dtype=I32)[:, None]).astype(BF16)
    o = _paged_call(
        kern, pt, [_group_rows(q, n_groups), glr, ck, cv, expand, view(win_k), view(win_v),
                   pad(ks_new), pad(vs_new), pad(kw_new), pad(vw_new), view(cache_ks), view(cache_vs)],
        2, F32, (n_groups, rows, HD),
        [pltpu.VMEM((2, past * n_groups, HD), F32), pltpu.VMEM((2, past * n_groups, HD), F32),
         pltpu.SemaphoreType.DMA((2,)), pltpu.SemaphoreType.DMA((2,))], "nsa_sample")
    return _ungroup_rows(o, nq)


def _tmajor(a):
    b, t, c = a.shape
    return jnp.swapaxes(a, 0, 1).reshape(t * b, c)


def _bmajor(a, b):
    tb, c = a.shape
    return jnp.swapaxes(a.reshape(tb // b, b, c), 0, 1)


def _col_splits(z, sizes):
    out, off = [], 0
    for s in sizes:
        out.append(z[..., off:off + s])
        off += s
    return out


def _row_tile(m, pref):
    return pref if m % pref == 0 else m


def kernel(x_prompt, x_sample, state_conv_a, cache_dsa_k, cache_dsa_v, cache_dsa_ik, cache_nsa_cmp_k, cache_nsa_cmp_v, cache_nsa_sel_k, cache_nsa_sel_v, cache_nsa_win_k, cache_nsa_win_v, state_conv_d, state_ffn_conv, cache_mem_k, cache_mem_v, page_table, mem_prompt, w_in_e, conv_a_w, w_out_e, w_in_o, nsa_pe_k, nsa_w1_k, nsa_w2_k, nsa_pe_v, nsa_w1_v, nsa_w2_v, conv_d_w, conv_d_b, ln_d_g, ln_d_b, w_out_o, norm_g, mem_norm_g, w_mq, w_mk, w_mv, w_mo, w_up, ffn_conv_w, w_down):
    bp, seq, d = x_prompt.shape
    nb, nq, _ = x_sample.shape
    assert bp == 1
    depth = norm_g.shape[0]
    ts = nb * nq
    d_a = conv_a_w.shape[-1]
    d_d = conv_d_w.shape[-1]
    d_ff = ffn_conv_w.shape[-1]
    kv_b = cache_dsa_k.shape[-2] * HD
    kv_c = cache_nsa_cmp_k.shape[-2] * HD
    n_kv_c = cache_nsa_cmp_k.shape[-2]
    dq_b = w_out_e.shape[1] - d_a
    dq_c = w_out_o.shape[1] - d_d
    n_gate = 3 * dq_c // HD
    split_e = (d_a, d_a, d_a, dq_b, kv_b, kv_b, H_IDX * D_IDX, D_IDX, H_IDX)
    split_o = (dq_c,) + (kv_c,) * 6 + (d_d, d_d, n_gate)
    gl_src = dq_c + 6 * kv_c
    w_in_o = jnp.concatenate([w_in_o[:, :, :gl_src], w_in_o[:, :, gl_src + n_gate:],
                              w_in_o[:, :, gl_src:gl_src + n_gate]], axis=2).astype(BF16)
    w_in_e, w_out_e, w_out_o, w_mq, w_mo, w_down = (
        w.astype(BF16) for w in (w_in_e, w_out_e, w_out_o, w_mq, w_mo, w_down))
    off_e = [sum(split_e[:j]) for j in range(len(split_e))]
    off_o = [sum(split_o[:j]) for j in range(len(split_o))]
    mem_g = mem_norm_g.reshape(depth, 1, d)
    tm_big = _row_tile(seq, 1024)
    tm_mid = _row_tile(seq, 512)
    wb = min(WINDOW, seq)

    yp = x_prompt.reshape(seq, d)
    ys = _tmajor(x_sample)
    mem = mem_prompt.reshape(mem_prompt.shape[1], d)
    names = ('p_conv_a', 'p_dsa_k', 'p_dsa_v', 'p_dsa_ik', 'p_cmp_k', 'p_cmp_v', 'p_sel_k', 'p_sel_v',
             'p_win_k', 'p_win_v', 'p_conv_d', 'p_ffn', 'p_mem_k', 'p_mem_v',
             's_conv_a', 's_dsa_k', 's_dsa_v', 's_dsa_ik', 's_cmp_k', 's_cmp_v', 's_sel_k', 's_sel_v',
             's_win_k', 's_win_v', 's_conv_d', 's_ffn')
    st = {n: [] for n in names}

    for l in range(depth):
        g = norm_g
        i = l // 2
        if l % 2 == 0:
            zp = norm_mm(yp, g, l, 0, w_in_e, i, tm=tm_big, tn=1024)
            zs = norm_mm(ys, g, l, 0, w_in_e, i, tm=ts, tn=512)
            kw_a = conv_a_w.shape[1]
            o_a, hist = gated_conv(zp, d_a, conv_a_w[i], jnp.zeros((_halo_rows(kw_a, 1), d_a), F32),
                                   tm=tm_mid, shift=1)
            st['p_conv_a'].append(hist[hist.shape[0] - (kw_a - 1):][None])
            k, v, ik = lax.optimization_barrier(
                tuple(zp[:, off_e[j]:off_e[j] + split_e[j]] for j in (4, 5, 7)))
            o_b = dsa_prompt(zp, off_e[3], off_e[6], off_e[7], dq_b // HD,
                             k.astype(BF16), v.astype(BF16), ik.astype(BF16))
            yp = mm_norm_res([o_a, o_b], w_out_e, i, g, l, 1, yp, tm=tm_mid, tn=512)
            st['p_dsa_k'].append(k.reshape(1, seq, -1, HD))
            st['p_dsa_v'].append(v.reshape(1, seq, -1, HD))
            st['p_dsa_ik'].append(ik[None])
            o_a, hist = gated_conv(zs, d_a, conv_a_w[i], _tmajor(state_conv_a[i]), tm=ts, shift=nb)
            st['s_conv_a'].append(_bmajor(hist, nb))
            _, _, _, q, k, v, qi, ik, wi = _col_splits(_bmajor(zs, nb), split_e)
            o_b = dsa_sample(q, qi, wi, k, v, ik, cache_dsa_k[i], cache_dsa_v[i], cache_dsa_ik[i],
                             page_table)
            ys = mm_norm_res([o_a, _tmajor(o_b)], w_out_e, i, g, l, 1, ys, tm=ts, tn=512)
            st['s_dsa_k'].append(k.reshape(nb, nq, -1, HD))
            st['s_dsa_v'].append(v.reshape(nb, nq, -1, HD))
            st['s_dsa_ik'].append(ik)
        else:
            zp = norm_mm(yp, g, l, 0, w_in_o, i, tm=tm_big, tn=1024)
            zs = norm_mm(ys, g, l, 0, w_in_o, i, tm=ts, tn=512)
            phi_k = (nsa_pe_k[i], nsa_w1_k[i], nsa_w2_k[i])
            phi_v = (nsa_pe_v[i], nsa_w1_v[i], nsa_w2_v[i])
            kw_d = conv_d_w.shape[1]
            kc, vc, ks, vs, kw, vw = lax.optimization_barrier(
                tuple(zp[:, off_o[j]:off_o[j] + kv_c] for j in range(1, 7)))
            rows2 = lambda a: a.reshape(1, seq * n_kv_c, HD)
            by_group = lambda c: c.reshape(-1, n_kv_c, HD).transpose(1, 0, 2).astype(BF16)
            ck = by_group(nsa_compress(rows2(kc), *phi_k, n_kv_c))
            cv = by_group(nsa_compress(rows2(vc), *phi_v, n_kv_c))
            o_c = nsa_prompt(zp, off_o[0], off_o[9], dq_c // HD, ck, cv, ks.astype(BF16),
                             vs.astype(BF16), kw.astype(BF16), vw.astype(BF16))
            o_d, hist = conformer_conv(zp, off_o[7], off_o[8], conv_d_w[i], conv_d_b[i], ln_d_g[i],
                                       ln_d_b[i], jnp.zeros((_halo_rows(kw_d, 1), d_d), F32),
                                       tm=tm_mid, shift=1)
            st['p_conv_d'].append(hist[hist.shape[0] - (kw_d - 1):][None])
            yp = mm_norm_res([o_c, o_d], w_out_o, i, g, l, 1, yp, tm=tm_mid, tn=512)
            kv4 = lambda a: a.reshape(1, -1, n_kv_c, HD)
            for n, a in zip(('p_cmp_k', 'p_cmp_v', 'p_sel_k', 'p_sel_v'), (kc, vc, ks, vs)):
                st[n].append(kv4(a))
            st['p_win_k'].append(kv4(kw[seq - wb:]))
            st['p_win_v'].append(kv4(vw[seq - wb:]))
            q, kc, vc, ks, vs, kw, vw, _, _, gl = _col_splits(_bmajor(zs, nb), split_o)
            ck = nsa_compress_paged(cache_nsa_cmp_k[i], page_table, *phi_k)
            cv = nsa_compress_paged(cache_nsa_cmp_v[i], page_table, *phi_v)
            win_k, win_v = cache_nsa_win_k[i], cache_nsa_win_v[i]
            o_c = nsa_sample(q, gl, ck, cv, win_k, win_v, ks, vs, kw, vw,
                             cache_nsa_sel_k[i], cache_nsa_sel_v[i], page_table)
            o_d, hist = conformer_conv(zs, off_o[7], off_o[8], conv_d_w[i], conv_d_b[i], ln_d_g[i],
                                       ln_d_b[i], _tmajor(state_conv_d[i]), tm=ts, shift=nb)
            st['s_conv_d'].append(_bmajor(hist, nb))
            ys = mm_norm_res([_tmajor(o_c), o_d], w_out_o, i, g, l, 1, ys, tm=ts, tn=512)
            kv4 = lambda a: a.reshape(nb, nq, n_kv_c, HD)
            for n, a in zip(('s_cmp_k', 's_cmp_v', 's_sel_k', 's_sel_v'), (kc, vc, ks, vs)):
                st[n].append(kv4(a))
            st['s_win_k'].append(jnp.concatenate([win_k, kv4(kw)], axis=1)[:, nq:])
            st['s_win_v'].append(jnp.concatenate([win_v, kv4(vw)], axis=1)[:, nq:])

        n_mem = mem.shape[0]
        mk = norm_mm(mem, mem_g, l, 0, w_mk, l, tm=n_mem, tn=512)
        mv = norm_mm(mem, mem_g, l, 0, w_mv, l, tm=n_mem, tn=512)
        st['p_mem_k'].append(mk.reshape(1, n_mem, -1, HD))
        st['p_mem_v'].append(mv.reshape(1, n_mem, -1, HD))
        yp = cross_attn(yp, g, l, 2, 3, w_mq, w_mo, l, mk, mv, tm=tm_mid)
        qs = norm_mm(ys, g, l, 2, w_mq, l, tm=ts, tn=512)
        a = mem_attn_batched(_bmajor(qs, nb), cache_mem_k, cache_mem_v, l)
        ys = mm_norm_res([_tmajor(a)], w_mo, l, g, l, 3, ys, tm=ts, tn=512)

        kw_f = ffn_conv_w.shape[1]
        gact, hist = ffn_up(yp, g, l, 4, w_up, l, ffn_conv_w[l],
                            jnp.zeros((_halo_rows(kw_f, 1), d_ff), F32), tm=tm_big, shift=1)
        st['p_ffn'].append(hist[hist.shape[0] - (kw_f - 1):, :d_ff][None])
        yp = mm_norm_res([gact], w_down, l, g, l, 5, yp, tm=tm_mid, tn=512, widths=(d_ff,))
        gact, hist = ffn_up(ys, g, l, 4, w_up, l, ffn_conv_w[l], _tmajor(state_ffn_conv[l]),
                            tm=ts, shift=nb)
        st['s_ffn'].append(_bmajor(hist[:, :d_ff], nb))
        ys = mm_norm_res([gact], w_down, l, g, l, 5, ys, tm=ts, tn=512, widths=(d_ff,))

    out = {n: jnp.stack(a) for n, a in st.items()}
    return (yp.reshape(1, seq, d), _bmajor(ys, nb)) + tuple(out[n] for n in names)
```

```python
import functools

import jax
import jax.numpy as jnp
from jax import lax
from jax.experimental import pallas as pl
from jax.experimental.pallas import tpu as pltpu

F32 = jnp.float32
BF16 = jnp.bfloat16
I32 = jnp.int32

EPS = 1e-6
NEG = -1e30
BIG = 1e4
HD = 128
PAGE = 128
CMP_BLK = 64
N_SEL = 16
WINDOW = 512
TOPK_MAX = 256
H_IDX = 16
D_IDX = 64
INT_MIN = -(2 ** 31)

VMEM_LIMIT_BYTES = 56 * 1024 * 1024


def _params(*sem):
    return pltpu.CompilerParams(dimension_semantics=sem, vmem_limit_bytes=VMEM_LIMIT_BYTES)


def _rms(x, g):
    return x * lax.rsqrt(jnp.mean(x * x, axis=-1, keepdims=True) + EPS) * g


def _dot(a, b):
    return jnp.dot(a.astype(BF16), b.astype(BF16), preferred_element_type=F32)


def _dot_nt(a, b):
    return lax.dot_general(a.astype(BF16), b.astype(BF16), (((1,), (1,)), ((), ())),
                           preferred_element_type=F32)


def _gain_spec(g, layer):
    return pl.BlockSpec((None,) + g.shape[1:], lambda *_: (layer, 0, 0))


def _norm_mm_kernel(x_ref, g_ref, w_ref, o_ref, xn_ref, *, gr):
    @pl.when(pl.program_id(1) == 0)
    def _():
        xn_ref[...] = _rms(x_ref[...], g_ref[gr:gr + 1, :]).astype(BF16)

    o_ref[...] = jnp.dot(xn_ref[...], w_ref[...].astype(BF16), preferred_element_type=F32)


def norm_mm(x, g, gl, gr, w, wl, *, tm, tn):
    m, k = x.shape
    n = w.shape[2]
    return pl.pallas_call(
        functools.partial(_norm_mm_kernel, gr=gr),
        grid=(m // tm, pl.cdiv(n, tn)),
        in_specs=[pl.BlockSpec((tm, k), lambda i, j: (i, 0)),
                  _gain_spec(g, gl),
                  pl.BlockSpec((None, k, tn), lambda i, j: (wl, 0, j))],
        out_specs=pl.BlockSpec((tm, tn), lambda i, j: (i, j)),
        out_shape=jax.ShapeDtypeStruct((m, n), F32),
        scratch_shapes=[pltpu.VMEM((tm, k), BF16)],
        compiler_params=_params("arbitrary", "arbitrary"),
        name="norm_mm",
    )(x, g, w)


def _mm_norm_res_kernel(*refs, n_in, widths, tn, nj, gr):
    a_refs = refs[:n_in]
    w_ref, g_ref, y_ref, o_ref, a_bf = refs[n_in:]
    j = pl.program_id(1)

    @pl.when(j == 0)
    def _():
        off = 0
        for a_ref, wd in zip(a_refs, widths):
            a_bf[:, off:off + wd] = a_ref[...].astype(BF16)
            off += wd

    col = pl.multiple_of(j * tn, tn)
    o_ref[:, pl.ds(col, tn)] = jnp.dot(a_bf[...], w_ref[...].astype(BF16),
                                       preferred_element_type=F32)

    @pl.when(j == nj - 1)
    def _():
        o_ref[...] = y_ref[...] + _rms(o_ref[...], g_ref[gr:gr + 1, :])


def mm_norm_res(a_list, w, wl, g, gl, gr, y, *, tm, tn, widths=None):
    m, n = y.shape
    widths = tuple(a.shape[1] for a in a_list) if widths is None else tuple(widths)
    k = sum(widths)
    assert w.shape[1:] == (k, n) and n % tn == 0
    nj = n // tn
    kern = functools.partial(_mm_norm_res_kernel, n_in=len(a_list), widths=widths, tn=tn, nj=nj,
                             gr=gr)
    return pl.pallas_call(
        kern,
        grid=(m // tm, nj),
        in_specs=[pl.BlockSpec((tm, wd), lambda i, j: (i, 0)) for wd in widths]
        + [pl.BlockSpec((None, k, tn), lambda i, j: (wl, 0, j)),
           _gain_spec(g, gl),
           pl.BlockSpec((tm, n), lambda i, j: (i, 0))],
        out_specs=pl.BlockSpec((tm, n), lambda i, j: (i, 0)),
        out_shape=jax.ShapeDtypeStruct((m, n), F32),
        scratch_shapes=[pltpu.VMEM((tm, k), BF16)],
        compiler_params=_params("arbitrary", "arbitrary"),
        name="mm_norm_res",
    )(*a_list, w, g, y)


def _order_key(x):
    b = pltpu.bitcast(x, I32)
    return b ^ ((b >> 31) & 0x7FFFFFFF)


def _kth_largest_key(count_ge, k, shape):
    t0 = jnp.where(count_ge(jnp.zeros(shape, I32)) >= k, 0, INT_MIN).astype(I32)

    def bit_body(n, t):
        cand = t | jnp.left_shift(jnp.int32(1), 30 - n)
        return jnp.where(count_ge(cand) >= k, cand, t)

    return lax.fori_loop(0, 31, bit_body, t0)


LOG2E = 1.4426950408889634
M_INIT = 0.1 * NEG


def _stage_queries(q_ref, qg_sc, n_groups, rep, tq):
    c = HD ** -0.5 * LOG2E
    for g in range(n_groups):
        for r in range(rep):
            h = g * rep + r
            qg_sc[g, r * tq:(r + 1) * tq, 0:HD] = (q_ref[:, h * HD:(h + 1) * HD] * c).astype(BF16)


def _flash_step(qg, kc, vc, bias, carry):
    m, l, acc = carry
    s = _dot_nt(qg, kc)
    if bias is not None:
        s = s + bias
    m_new = jnp.maximum(m, jnp.max(s, axis=1, keepdims=True))
    p = jnp.exp2(s - m_new)
    alpha = jnp.exp2(m - m_new)
    return (m_new, alpha * l + jnp.sum(p, axis=1, keepdims=True),
            alpha * acc + jnp.dot(p.astype(BF16), vc, preferred_element_type=F32))


def _flash_init(rows):
    return (jnp.full((rows, 1), M_INIT, F32), jnp.zeros((rows, 1), F32), jnp.zeros((rows, HD), F32))


def _flash_finish(carry):
    _, l, acc = carry
    return acc / jnp.where(l > 0, l, 1.0)


def _dsa_prompt_kernel(q_ref, qia_ref, qib_ref, iw_ref, k_ref, v_ref, ik_ref, o_ref,
                       key_sc, qs_sc, wb_sc, qg_sc, *, tq, tks, tka, topk, n_heads, n_groups):
    i = pl.program_id(0)
    q0 = i * tq
    nch_a = (q0 + tq + tka - 1) // tka
    nch_s = nch_a * (tka // tks)
    rep = n_heads // n_groups
    qpos = q0 + lax.broadcasted_iota(I32, (tq, 1), 0)

    half = H_IDX // 2
    wscale = H_IDX ** -0.5 * D_IDX ** -0.5
    for h in range(H_IDX):
        src = qia_ref if h < half else qib_ref
        hh = h % half
        qs_sc[h * tq:(h + 1) * tq, :] = src[:, hh * D_IDX:(hh + 1) * D_IDX].astype(BF16)
        wb_sc[h] = jnp.broadcast_to(iw_ref[:, D_IDX + h:D_IDX + h + 1] * wscale, (tq, 128))
    _stage_queries(q_ref, qg_sc, n_groups, rep, tq)

    def score_body(c, carry):
        col = pl.multiple_of(c * tks, tks)
        ikc = ik_ref[:, pl.ds(col, tks)]
        acc = jnp.zeros((tq, tks), F32)
        for h in range(H_IDX):
            s = jnp.dot(qs_sc[h * tq:(h + 1) * tq, :], ikc, preferred_element_type=F32)
            acc = acc + jnp.concatenate([wb_sc[h]] * (tks // 128), axis=1) * jnp.maximum(s, 0.0)
        kpos = col + lax.broadcasted_iota(I32, (1, tks), 1)
        key_sc[:, pl.ds(col, tks)] = jnp.where(kpos <= qpos, _order_key(acc), INT_MIN)
        return carry

    lax.fori_loop(0, nch_s, score_body, 0)

    def count_ge(t):
        tb = jnp.broadcast_to(t, (tq, 128))

        def body(c, cnt):
            col = pl.multiple_of(c * tka, tka)
            for j in range(tka // 128):
                cnt = cnt + jnp.where(key_sc[:, pl.ds(col + j * 128, 128)] >= tb, 1, 0)
            return cnt

        cnt = lax.fori_loop(0, nch_a, body, jnp.zeros((tq, 128), I32))
        return jnp.sum(cnt, axis=1, keepdims=True)

    thr = jnp.maximum(_kth_largest_key(count_ge, topk, (tq, 1)), INT_MIN + 1)

    def bias_body(c, carry):
        col = pl.multiple_of(c * tks, tks)
        bias = jnp.where(key_sc[:, pl.ds(col, tks)] >= thr, 0.0, NEG)
        key_sc[:, pl.ds(col, tks)] = pltpu.bitcast(bias, I32)
        return carry

    lax.fori_loop(0, nch_s, bias_body, 0)

    def att_body(c, carry):
        col = pl.multiple_of(c * tka, tka)
        bias = pltpu.bitcast(key_sc[:, pl.ds(col, tka)], F32)
        bias = jnp.concatenate([bias] * rep, axis=0)
        out = []
        for g in range(n_groups):
            kc = k_ref[pl.ds(col, tka), g * HD:(g + 1) * HD]
            vc = v_ref[pl.ds(col, tka), g * HD:(g + 1) * HD]
            out.append(_flash_step(qg_sc[g], kc, vc, bias, carry[g]))
        return tuple(out)

    res = lax.fori_loop(0, nch_a, att_body, tuple(_flash_init(rep * tq) for _ in range(n_groups)))
    for g in range(n_groups):
        o = _flash_finish(res[g])
        for r in range(rep):
            o_ref[:, (g * rep + r) * HD:(g * rep + r + 1) * HD] = o[r * tq:(r + 1) * tq, :]


def dsa_prompt(z, q_off, qi_off, iw_off, n_heads, k_bf, v_bf, ik_bf, *, tq=128, tks=256, tka=512):
    t = z.shape[0]
    dq = n_heads * HD
    half = H_IDX * D_IDX // 2
    n_groups = k_bf.shape[1] // HD
    assert q_off % dq == 0 and qi_off % half == 0 and iw_off % 128 == 0 and D_IDX + H_IDX <= 128
    assert t % tka == 0 and tka % tks == 0
    topk = min(TOPK_MAX, t // 4)
    kern = functools.partial(_dsa_prompt_kernel, tq=tq, tks=tks, tka=tka, topk=topk,
                             n_heads=n_heads, n_groups=n_groups)
    full = lambda a: pl.BlockSpec(a.shape, lambda i: (0, 0))
    col = lambda w, j: pl.BlockSpec((tq, w), lambda i: (i, j))
    return pl.pallas_call(
        kern,
        grid=(t // tq,),
        in_specs=[col(dq, q_off // dq), col(half, qi_off // half), col(half, qi_off // half + 1),
                  col(128, iw_off // 128), full(k_bf), full(v_bf), full(ik_bf)],
        out_specs=pl.BlockSpec((tq, dq), lambda i: (i, 0)),
        out_shape=jax.ShapeDtypeStruct((t, dq), F32),
        scratch_shapes=[pltpu.VMEM((tq, t), I32), pltpu.VMEM((H_IDX * tq, D_IDX), BF16),
                        pltpu.VMEM((H_IDX, tq, 128), F32),
                        pltpu.VMEM((n_groups, n_heads // n_groups * tq, HD), BF16)],
        compiler_params=_params("arbitrary"),
        name="dsa_prompt",
    )(z, z, z, z, k_bf, v_bf, ik_bf)


def _nsa_compress_kernel(x_ref, pe_ref, w1_ref, w2_ref, o_ref, *, nblk, n_groups):
    _compress_rows(x_ref, pe_ref, w1_ref, w2_ref, o_ref, nblk=nblk, n_groups=n_groups)


def _compress_rows(x_ref, pe_ref, w1_ref, w2_ref, o_ref, *, nblk, n_groups):
    acc = jnp.zeros((n_groups * nblk, HD), F32)
    for p in range(CMP_BLK):
        pe_p = pe_ref[p:p + 1, :]
        lhs = jnp.concatenate(
            [x_ref[pl.ds(p * n_groups + g, nblk, stride=CMP_BLK * n_groups), :] + pe_p
             for g in range(n_groups)], axis=0)
        acc = acc + _dot(lhs, w1_ref[p * HD:(p + 1) * HD, :])
    out = _dot(jnp.maximum(acc, 0.0), w2_ref[...])
    for g in range(n_groups):
        o_ref[:, g * HD:(g + 1) * HD] = out[g * nblk:(g + 1) * nblk, :]


def nsa_compress(x, pe, w1, w2, n_groups):
    b, tg, _ = x.shape
    nblk = tg // (n_groups * CMP_BLK)
    kern = functools.partial(_nsa_compress_kernel, nblk=nblk, n_groups=n_groups)
    full = lambda a: pl.BlockSpec(a.shape, lambda i: (0,) * a.ndim)
    return pl.pallas_call(
        kern,
        grid=(b,),
        in_specs=[pl.BlockSpec((None, tg, HD), lambda i: (i, 0, 0)), full(pe), full(w1), full(w2)],
        out_specs=pl.BlockSpec((None, nblk, n_groups * HD), lambda i: (i, 0, 0)),
        out_shape=jax.ShapeDtypeStruct((b, nblk, n_groups * HD), F32),
        compiler_params=_params("arbitrary"),
        name="nsa_compress",
    )(x, pe, w1, w2)


def _masked_softmax(s, ok):
    s = jnp.where(ok, s, NEG)
    e = jnp.where(ok, jnp.exp(s - jnp.max(s, axis=-1, keepdims=True)), 0.0)
    d = jnp.sum(e, axis=-1, keepdims=True)
    return e / jnp.where(d > 0, d, 1.0)


def _top_blocks(score, n_sel, axis=1):
    nb = score.shape[axis]
    pos = lax.broadcasted_iota(I32, score.shape, axis)
    sel = jnp.zeros(score.shape, F32)
    for _ in range(n_sel):
        m = jnp.max(score, axis=axis, keepdims=True)
        idx = jnp.min(jnp.where(score == m, pos, nb), axis=axis, keepdims=True)
        hit = pos == idx
        sel = jnp.where(hit, 1.0, sel)
        score = jnp.where(hit, -3e38, score)
    return sel


def _masked_softmax2(s, ok):
    s = jnp.where(ok, s, NEG)
    e = jnp.where(ok, jnp.exp2(s - jnp.max(s, axis=-1, keepdims=True)), 0.0)
    d = jnp.sum(e, axis=-1, keepdims=True)
    return e / jnp.where(d > 0, d, 1.0)


def _nsa_prompt_kernel(q_ref, gl_ref, ck_ref, cv_ref, ks_ref, vs_ref, kw_ref, vw_ref, o_ref,
                       qg_sc, *, tq, tk, n_heads, n_groups, nbc, gl_lane):
    i = pl.program_id(0)
    q0 = i * tq
    rep = n_heads // n_groups
    qpos = q0 + lax.broadcasted_iota(I32, (tq, 1), 0)
    jb = lax.broadcasted_iota(I32, (1, nbc), 1)
    cmp_ok = jnp.concatenate([jnp.where(((jb + 1) * CMP_BLK - 1) <= qpos, 1, 0)] * rep, axis=0) > 0
    cur = qpos // CMP_BLK
    gates = jax.nn.sigmoid(gl_ref[:, gl_lane:gl_lane + 3 * n_heads])

    _stage_queries(q_ref, qg_sc, n_groups, rep, tq)
    w0 = pl.multiple_of(jnp.maximum(q0 - WINDOW, 0), tq)
    wlen = WINDOW + tq
    diff = qpos - (w0 + lax.broadcasted_iota(I32, (1, wlen), 1))
    w_ok = jnp.where((diff >= 0) & (diff <= WINDOW), 1, 0)
    w_ok = jnp.concatenate([w_ok] * rep, axis=0) > 0
    imps = []
    for g in range(n_groups):
        qg = qg_sc[g, :, 0:HD]
        p_c = _masked_softmax2(_dot_nt(qg, ck_ref[g]), cmp_ok)
        o_c = _dot(p_c, cv_ref[g])
        imp = p_c[0:tq]
        for r in range(1, rep):
            imp = imp + p_c[r * tq:(r + 1) * tq]
        imps.append(imp)
        kw = kw_ref[pl.ds(w0, wlen), g * HD:(g + 1) * HD]
        vw = vw_ref[pl.ds(w0, wlen), g * HD:(g + 1) * HD]
        o_w = _dot(_masked_softmax2(_dot_nt(qg, kw), w_ok), vw)
        for r in range(rep):
            h = g * rep + r
            rows = slice(r * tq, (r + 1) * tq)
            o_ref[:, h * HD:(h + 1) * HD] = (gates[:, 3 * h:3 * h + 1] * o_c[rows]
                                             + gates[:, 3 * h + 2:3 * h + 3] * o_w[rows])

    imp_t = jnp.concatenate([imp.T for imp in imps], axis=1)
    jb_t = lax.broadcasted_iota(I32, (nbc, 1), 0)
    cur_t = (q0 + lax.broadcasted_iota(I32, (1, tq), 1)) // CMP_BLK
    cur_t = jnp.concatenate([cur_t] * n_groups, axis=1)
    blk_score = jnp.where(jb_t > cur_t, NEG, jnp.where((jb_t == cur_t) | (jb_t == 0), BIG, imp_t))
    sel_t = _top_blocks(blk_score, min(N_SEL, nbc), axis=0)
    for g in range(n_groups):
        pen = ((sel_t[:, g * tq:(g + 1) * tq].T - 1.0) * (-NEG)).astype(BF16)
        for r in range(rep):
            qg_sc[g, r * tq:(r + 1) * tq, HD:HD + nbc] = pen

    nch = (q0 + tq + tk - 1) // tk
    kw_aug = HD + nbc

    def chunk(c, carry, bias):
        col = pl.multiple_of(c * tk, tk)
        out = []
        for g in range(n_groups):
            kc = ks_ref[pl.ds(col, tk), g * kw_aug:(g + 1) * kw_aug]
            vc = vs_ref[pl.ds(col, tk), g * HD:(g + 1) * HD]
            out.append(_flash_step(qg_sc[g], kc, vc, bias, carry[g]))
        return tuple(out)

    init = tuple(_flash_init(rep * tq) for _ in range(n_groups))
    res = lax.fori_loop(0, nch - 1, lambda c, carry: chunk(c, carry, None), init)
    kpos = (nch - 1) * tk + lax.broadcasted_iota(I32, (1, tk), 1)
    causal = jnp.concatenate([jnp.where(kpos <= qpos, 0.0, NEG)] * rep, axis=0)
    res = chunk(nch - 1, res, causal)

    for g in range(n_groups):
        o_s = _flash_finish(res[g])
        for r in range(rep):
            h = g * rep + r
            o_ref[:, h * HD:(h + 1) * HD] += gates[:, 3 * h + 1:3 * h + 2] * o_s[r * tq:(r + 1) * tq]


def nsa_prompt(z, q_off, gl_off, n_heads, ck_bf, cv_bf, ks_bf, vs_bf, kw_bf, vw_bf, *, tq=128, tk=512):
    t = z.shape[0]
    dq = n_heads * HD
    n_groups, nbc, _ = ck_bf.shape
    gl_lane = gl_off % 128
    assert t >= WINDOW + tq and t % tk == 0 and nbc == t // CMP_BLK
    assert q_off % dq == 0 and gl_lane + 3 * n_heads <= 128
    kern = functools.partial(_nsa_prompt_kernel, tq=tq, tk=tk, n_heads=n_heads,
                             n_groups=n_groups, nbc=nbc, gl_lane=gl_lane)
    onehot = (jnp.arange(t, dtype=I32)[:, None] // CMP_BLK
              == jnp.arange(nbc, dtype=I32)[None, :]).astype(BF16)
    ks_aug = jnp.concatenate(
        [a for g in range(n_groups) for a in (ks_bf[:, g * HD:(g + 1) * HD], onehot)], axis=1)
    full = lambda a: pl.BlockSpec(a.shape, lambda i: (0,) * a.ndim)
    col = lambda w, j: pl.BlockSpec((tq, w), lambda i: (i, j))
    return pl.pallas_call(
        kern,
        grid=(t // tq,),
        in_specs=[col(dq, q_off // dq), col(128, gl_off // 128), full(ck_bf), full(cv_bf),
                  full(ks_aug), full(vs_bf), full(kw_bf), full(vw_bf)],
        out_specs=pl.BlockSpec((tq, dq), lambda i: (i, 0)),
        out_shape=jax.ShapeDtypeStruct((t, dq), F32),
        scratch_shapes=[pltpu.VMEM((n_groups, n_heads // n_groups * tq, HD + nbc), BF16)],
        compiler_params=_params("arbitrary"),
        name="nsa_prompt",
    )(z, z, ck_bf, cv_bf, ks_aug, vs_bf, kw_bf, vw_bf)


def _halo_rows(kw, shift):
    return max(8, (kw - 1) * shift)


SUBLANES = 8


def _dwconv(ext_ref, w_ref, u, prev_ref, first, *, tm, kw, shift, part_ref=None):
    hp = _halo_rows(kw, shift)
    base = hp - (kw - 1) * shift

    @pl.when(first)
    def _():
        ext_ref[0:hp, :] = prev_ref[...]
        if part_ref is not None:
            ext_ref[hp + tm:hp + tm + SUBLANES, :] = jnp.zeros((SUBLANES, ext_ref.shape[1]), F32)

    ext_ref[hp:hp + tm, :] = u
    y = None
    if part_ref is None:
        for i in range(kw):
            term = w_ref[i:i + 1, :] * ext_ref[pl.ds(base + i * shift, tm), :]
            y = term if y is None else y + term
    else:
        assert shift == 1 and base % SUBLANES == 0
        for b in range(min(SUBLANES, kw)):
            part = None
            for i in range(b, kw, SUBLANES):
                term = w_ref[i:i + 1, :] * ext_ref[pl.ds(base + i - b, tm + SUBLANES), :]
                part = term if part is None else part + term
            if b == 0:
                y = part[0:tm]
            else:
                part_ref[...] = part
                y = y + part_ref[pl.ds(b, tm), :]
    tail = ext_ref[tm:tm + hp, :]
    ext_ref[0:hp, :] = tail
    return y, tail


def _gated_conv_kernel(xa_ref, bg_ref, cg_ref, w_ref, prev_ref, o_ref, st_ref, ext_ref,
                       *, tm, kw, shift):
    u = cg_ref[...] * xa_ref[...]
    y, tail = _dwconv(ext_ref, w_ref, u, prev_ref, pl.program_id(0) == 0, tm=tm, kw=kw, shift=shift)
    o_ref[...] = bg_ref[...] * y
    st_ref[...] = tail


def gated_conv(z, c, w, prev, *, tm, shift):
    m = z.shape[0]
    kw = w.shape[0]
    hp = _halo_rows(kw, shift)
    kern = functools.partial(_gated_conv_kernel, tm=tm, kw=kw, shift=shift)
    col = lambda j: pl.BlockSpec((tm, c), lambda i: (i, j))
    return pl.pallas_call(
        kern,
        grid=(m // tm,),
        in_specs=[col(0), col(1), col(2), pl.BlockSpec((kw, c), lambda i: (0, 0)),
                  pl.BlockSpec((hp, c), lambda i: (0, 0))],
        out_specs=[pl.BlockSpec((tm, c), lambda i: (i, 0)), pl.BlockSpec((hp, c), lambda i: (0, 0))],
        out_shape=[jax.ShapeDtypeStruct((m, c), F32), jax.ShapeDtypeStruct((hp, c), F32)],
        scratch_shapes=[pltpu.VMEM((hp + tm, c), F32)],
        compiler_params=_params("arbitrary"),
        name="gated_conv",
    )(z, z, z, w, prev)


def _conformer_kernel(dpa_ref, dpb_ref, dga_ref, dgb_ref, w_ref, b_ref, lg_ref, lb_ref, prev_ref,
                      o_ref, st_ref, ext_ref, *part_ref, tm, kw, shift):
    u = jnp.concatenate([dpa_ref[...] * jax.nn.sigmoid(dga_ref[...]),
                         dpb_ref[...] * jax.nn.sigmoid(dgb_ref[...])], axis=1)
    c, tail = _dwconv(ext_ref, w_ref, u, prev_ref, pl.program_id(0) == 0, tm=tm, kw=kw, shift=shift,
                      part_ref=part_ref[0] if part_ref else None)
    c = c + b_ref[...]
    mu = jnp.mean(c, axis=-1, keepdims=True)
    xc = c - mu
    y = xc * lax.rsqrt(jnp.mean(xc * xc, axis=-1, keepdims=True) + EPS) * lg_ref[...] + lb_ref[...]
    o_ref[...] = y * jax.nn.sigmoid(y)
    st_ref[...] = tail


def conformer_conv(z, dp_off, dg_off, w, b, ln_g, ln_b, prev, *, tm, shift):
    m = z.shape[0]
    kw, c = w.shape
    half = c // 2
    assert dp_off % half == 0 and dg_off % half == 0
    hp = _halo_rows(kw, shift)
    kern = functools.partial(_conformer_kernel, tm=tm, kw=kw, shift=shift)
    col = lambda off, j: pl.BlockSpec((tm, half), lambda i: (i, off // half + j))
    row = pl.BlockSpec((tm, c), lambda i: (i, 0))
    vec = pl.BlockSpec((1, c), lambda i: (0, 0))
    grouped = shift == 1 and (hp - (kw - 1)) % SUBLANES == 0
    scratch = ([pltpu.VMEM((hp + tm + SUBLANES, c), F32), pltpu.VMEM((tm + SUBLANES, c), F32)]
               if grouped else [pltpu.VMEM((hp + tm, c), F32)])
    return pl.pallas_call(
        kern,
        grid=(m // tm,),
        in_specs=[col(dp_off, 0), col(dp_off, 1), col(dg_off, 0), col(dg_off, 1),
                  pl.BlockSpec((kw, c), lambda i: (0, 0)), vec, vec, vec,
                  pl.BlockSpec((hp, c), lambda i: (0, 0))],
        out_specs=[row, pl.BlockSpec((hp, c), lambda i: (0, 0))],
        out_shape=[jax.ShapeDtypeStruct((m, c), F32), jax.ShapeDtypeStruct((hp, c), F32)],
        scratch_shapes=scratch,
        compiler_params=_params("arbitrary"),
        name="conformer_conv",
    )(z, z, z, z, w, b.reshape(1, c), ln_g.reshape(1, c), ln_b.reshape(1, c), prev)


FFN_TF = 128
FFN_NB = 4


def _ffn_up_kernel(*refs, tm, kw, shift, gr):
    y_ref, g_ref = refs[:2]
    wa_refs = refs[2:2 + FFN_NB]
    wv_refs = refs[2 + FFN_NB:2 + 2 * FFN_NB]
    cw_ref, prev_ref, o_ref, st_ref, xn_ref, wcat_ref, halo_ref, ext_ref = refs[2 + 2 * FFN_NB:]
    i = pl.program_id(0)
    j = pl.program_id(1)
    hp = _halo_rows(kw, shift)
    tf = FFN_NB * FFN_TF

    @pl.when(j == 0)
    def _():
        xn_ref[...] = _rms(y_ref[...], g_ref[gr:gr + 1, :]).astype(BF16)

    @pl.when(i == 0)
    def _():
        ext_ref[0:hp, :] = prev_ref[...]

    @pl.when(i > 0)
    def _():
        ext_ref[0:hp, :] = halo_ref[j]

    def granule_matmul(s):
        wcat_ref[s, :, 0:FFN_TF] = wa_refs[s][...].astype(BF16)
        wcat_ref[s, :, FFN_TF:2 * FFN_TF] = wv_refs[s][...].astype(BF16)
        return jnp.dot(xn_ref[...], wcat_ref[s], preferred_element_type=F32)

    h_next = granule_matmul(0)
    for s in range(FFN_NB):
        cols = slice(s * FFN_TF, (s + 1) * FFN_TF)
        h = h_next
        if s + 1 < FFN_NB:
            h_next = granule_matmul(s + 1)
        ext_ref[hp:hp + tm, cols] = h[:, 0:FFN_TF]
        c = None
        for t in range(kw):
            term = cw_ref[t:t + 1, cols] * ext_ref[pl.ds(hp - (kw - 1 - t) * shift, tm), cols]
            c = term if c is None else c + term
        o_ref[:, cols] = (c * jax.nn.sigmoid(c) * h[:, FFN_TF:2 * FFN_TF]).astype(BF16)
    tail = ext_ref[tm:tm + hp, :]
    halo_ref[j] = tail
    st_ref[...] = tail


def ffn_up(y, g, gl, gr, w_up, wl, conv_w, prev, *, tm, shift):
    m, d = y.shape
    kw, f = conv_w.shape
    nf = f // FFN_TF
    assert f % FFN_TF == 0 and w_up.shape[1:] == (d, 2 * f)
    tf = FFN_NB * FFN_TF
    nj = pl.cdiv(nf, FFN_NB)
    fp = nj * tf
    hp = _halo_rows(kw, shift)
    nm = m // tm
    conv_w = jnp.pad(conv_w, ((0, 0), (0, fp - f)))
    prev = jnp.pad(prev, ((0, 0), (0, fp - f)))
    kern = functools.partial(_ffn_up_kernel, tm=tm, kw=kw, shift=shift, gr=gr)

    def granule(base, s):
        return pl.BlockSpec((None, d, FFN_TF),
                            lambda i, j: (wl, 0, base + jnp.minimum(FFN_NB * j + s, nf - 1)))

    return pl.pallas_call(
        kern,
        grid=(nm, nj),
        in_specs=[pl.BlockSpec((tm, d), lambda i, j: (i, 0)), _gain_spec(g, gl)]
        + [granule(0, s) for s in range(FFN_NB)] + [granule(nf, s) for s in range(FFN_NB)]
        + [pl.BlockSpec((kw, tf), lambda i, j: (0, j)), pl.BlockSpec((hp, tf), lambda i, j: (0, j))],
        out_specs=[pl.BlockSpec((tm, tf), lambda i, j: (i, j)),
                   pl.BlockSpec((hp, tf), lambda i, j: (i, j))],
        out_shape=[jax.ShapeDtypeStruct((m, fp), BF16), jax.ShapeDtypeStruct((nm * hp, fp), F32)],
        scratch_shapes=[pltpu.VMEM((tm, d), BF16), pltpu.VMEM((FFN_NB, d, 2 * FFN_TF), BF16),
                        pltpu.VMEM((nj, hp, tf), F32), pltpu.VMEM((hp + tm, tf), F32)],
        compiler_params=_params("arbitrary", "arbitrary"),
        name="ffn_up",
    )(y, g, *([w_up] * (2 * FFN_NB)), conv_w, prev)


def _mem_heads(q, mk_head, mv_head, n_heads):
    scale = HD ** -0.5
    outs = []
    for h in range(n_heads):
        s = _dot_nt(q[:, h * HD:(h + 1) * HD], mk_head(h)) * scale
        e = jnp.exp(s - jnp.max(s, axis=-1, keepdims=True))
        p = e / jnp.sum(e, axis=-1, keepdims=True)
        outs.append(_dot(p, mv_head(h)))
    return jnp.concatenate(outs, axis=1)


def _xattn_kernel(y_ref, g_ref, wq_ref, mk_ref, mv_ref, wo_ref, o_ref, wq_bf, wo_bf,
                  *, n_heads, gr_in, gr_out):
    @pl.when(pl.program_id(0) == 0)
    def _():
        wq_bf[...] = wq_ref[...].astype(BF16)
        wo_bf[...] = wo_ref[...].astype(BF16)

    y = y_ref[...]
    q = jnp.dot(_rms(y, g_ref[gr_in:gr_in + 1, :]).astype(BF16), wq_bf[...],
                preferred_element_type=F32)
    o = _mem_heads(q, lambda h: mk_ref[:, h * HD:(h + 1) * HD],
                   lambda h: mv_ref[:, h * HD:(h + 1) * HD], n_heads)
    f = jnp.dot(o.astype(BF16), wo_bf[...], preferred_element_type=F32)
    o_ref[...] = y + _rms(f, g_ref[gr_out:gr_out + 1, :])


def cross_attn(y, g, gl, gr_in, gr_out, wq, wo, wl, mk, mv, *, tm):
    m, d = y.shape
    dh = wq.shape[2]
    kern = functools.partial(_xattn_kernel, n_heads=dh // HD, gr_in=gr_in, gr_out=gr_out)
    full = lambda a: pl.BlockSpec(a.shape, lambda i: (0, 0))
    layer = lambda a: pl.BlockSpec((None,) + a.shape[1:], lambda i: (wl, 0, 0))
    row = pl.BlockSpec((tm, d), lambda i: (i, 0))
    return pl.pallas_call(
        kern,
        grid=(m // tm,),
        in_specs=[row, _gain_spec(g, gl), layer(wq), full(mk), full(mv), layer(wo)],
        out_specs=row,
        out_shape=jax.ShapeDtypeStruct((m, d), F32),
        scratch_shapes=[pltpu.VMEM(wq.shape[1:], BF16), pltpu.VMEM(wo.shape[1:], BF16)],
        compiler_params=_params("arbitrary"),
        name="cross_attn",
    )(y, g, wq, mk, mv, wo)


def _mem_attn_kernel(q_ref, mk_ref, mv_ref, o_ref, *, n_heads, bb):
    for j in range(bb):
        o_ref[j] = _mem_heads(q_ref[j], lambda h: mk_ref[j, :, h, :], lambda h: mv_ref[j, :, h, :],
                              n_heads)


def mem_attn_batched(q, mk, mv, layer, *, bb=4):
    b, t, dh = q.shape
    bb = bb if b % bb == 0 else 1
    kern = functools.partial(_mem_attn_kernel, n_heads=dh // HD, bb=bb)
    cache = pl.BlockSpec((None, bb) + mk.shape[2:], lambda i: (layer, i, 0, 0, 0))
    blk = pl.BlockSpec((bb, t, dh), lambda i: (i, 0, 0))
    return pl.pallas_call(
        kern,
        grid=(b // bb,),
        in_specs=[blk, cache, cache],
        out_specs=blk,
        out_shape=jax.ShapeDtypeStruct(q.shape, F32),
        compiler_params=_params("arbitrary"),
        name="mem_attn_batched",
    )(q, mk, mv)


def _page_copy(cache, pt_ref, b, p, buf, slot, sem):
    rows, lanes = cache.shape[1:]
    if buf.shape[2] == lanes:
        dst = buf.at[slot, pl.ds(p * rows, rows), :]
    else:
        dst = buf.at[slot, :, pl.ds(pl.multiple_of(p * lanes, lanes), lanes)]
    return pltpu.make_async_copy(cache.at[pt_ref[b, p]], dst, sem.at[slot])


def _for_pages(caches, bufs, sems, pt_ref, b, slot, n_pages, fn):
    def body(p, carry):
        for cache, buf, sem in zip(caches, bufs, sems):
            fn(_page_copy(cache, pt_ref, b, p, buf, slot, sem))
        return carry

    lax.fori_loop(0, n_pages, body, 0)


def _fetch_pages(caches, bufs, sems, pt_ref, n_pages):
    b = pl.program_id(0)
    slot = b % 2
    args = (caches, bufs, sems, pt_ref)

    @pl.when(b == 0)
    def _():
        _for_pages(*args, b, slot, n_pages, lambda cp: cp.start())

    @pl.when(b + 1 < pl.num_programs(0))
    def _():
        _for_pages(*args, b + 1, 1 - slot, n_pages, lambda cp: cp.start())

    _for_pages(*args, b, slot, n_pages, lambda cp: cp.wait())
    return slot


def _paged_call(kern, pt, inputs, n_any, out_shape, out_block, scratch, name):
    nb = pt.shape[0]

    def spec(a):
        nd = a.ndim - 1
        return pl.BlockSpec((None,) + a.shape[1:], lambda i, pt_ref: (i,) + (0,) * nd)

    def shared(a):
        nd = a.ndim
        return pl.BlockSpec(a.shape, lambda i, pt_ref: (0,) * nd)

    in_specs = []
    for a in inputs[:len(inputs) - n_any]:
        in_specs.append(spec(a) if a.shape[0] == nb and a.ndim >= 3 else shared(a))
    in_specs += [pl.BlockSpec(memory_space=pl.ANY)] * n_any
    nd_o = len(out_block)
    return pl.pallas_call(
        kern,
        grid_spec=pltpu.PrefetchScalarGridSpec(
            num_scalar_prefetch=1,
            grid=(nb,),
            in_specs=in_specs,
            out_specs=pl.BlockSpec((None,) + tuple(out_block), lambda i, pt_ref: (i,) + (0,) * nd_o),
            scratch_shapes=scratch),
        out_shape=jax.ShapeDtypeStruct((nb,) + tuple(out_block), out_shape),
        compiler_params=_params("arbitrary"),
        name=name,
    )(pt, *inputs)


def _attend_two_parts(qg, k_a, v_a, ok_a, k_b, v_b, ok_b):
    scale = HD ** -0.5
    s_a = jnp.where(ok_a, _dot_nt(qg, k_a) * scale, NEG)
    s_b = jnp.where(ok_b, _dot_nt(qg, k_b) * scale, NEG)
    mx = jnp.maximum(jnp.max(s_a, axis=1, keepdims=True), jnp.max(s_b, axis=1, keepdims=True))
    e_a = jnp.where(ok_a, jnp.exp(s_a - mx), 0.0)
    e_b = jnp.where(ok_b, jnp.exp(s_b - mx), 0.0)
    den = jnp.sum(e_a, axis=1, keepdims=True) + jnp.sum(e_b, axis=1, keepdims=True)
    return (_dot(e_a, v_a) + _dot(e_b, v_b)) / jnp.where(den > 0, den, 1.0)


def _dsa_sample_select_kernel(pt_ref, qi_ref, wi_ref, ikn_ref, ik_hbm, m_ref, ikbuf, key_sc, sem,
                              *, n_pages, tk, topk, n_new, nq, rows):
    slot = _fetch_pages([ik_hbm], [ikbuf], [sem], pt_ref, n_pages)
    past = n_pages * PAGE
    qi = qi_ref[...].astype(BF16)
    w = wi_ref[...] * (H_IDX ** -0.5 * D_IDX ** -0.5)

    def scores(ikc_t):
        r = jnp.maximum(_dot(qi, ikc_t), 0.0) * w
        return jnp.sum(r.reshape(H_IDX, rows, ikc_t.shape[1]), axis=0)

    def score_body(c, carry):
        col = pl.multiple_of(c * tk, tk)
        key_sc[:, pl.ds(col, tk)] = _order_key(scores(ikbuf[slot, :, pl.ds(col, tk)]))
        return carry

    lax.fori_loop(0, past // tk, score_body, 0)
    t_row = lax.rem(lax.broadcasted_iota(I32, (rows, PAGE), 0), nq)
    col = lax.broadcasted_iota(I32, (rows, PAGE), 1)
    new_ok = (col < n_new) & (col <= t_row)
    key_sc[:, past:past + PAGE] = jnp.where(new_ok, _order_key(scores(ikn_ref[...])), INT_MIN)

    def count_ge(t):
        return jnp.sum(jnp.where(key_sc[...] >= t, 1, 0), axis=1, keepdims=True)

    thr = jnp.maximum(_kth_largest_key(count_ge, topk, (rows, 1)), INT_MIN + 1)
    m_ref[...] = jnp.where(key_sc[...] >= thr, 1.0, 0.0)


def _dsa_sample_attn_kernel(pt_ref, q_ref, m_ref, kn_ref, vn_ref, k_hbm, v_hbm, o_ref,
                            kbuf, vbuf, ksem, vsem, *, n_pages, n_groups):
    slot = _fetch_pages([k_hbm, v_hbm], [kbuf, vbuf], [ksem, vsem], pt_ref, n_pages)
    past = n_pages * PAGE
    tile = q_ref.shape[1] // m_ref.shape[0]
    ok_c = jnp.concatenate([m_ref[:, 0:past]] * tile, axis=0) > 0.5
    ok_n = jnp.concatenate([m_ref[:, past:past + PAGE]] * tile, axis=0) > 0.5
    for g in range(n_groups):
        o_ref[g] = _attend_two_parts(
            q_ref[g].astype(BF16),
            kbuf[slot, pl.ds(g, past, stride=n_groups), :],
            vbuf[slot, pl.ds(g, past, stride=n_groups), :], ok_c,
            kn_ref[pl.ds(g, PAGE, stride=n_groups), :],
            vn_ref[pl.ds(g, PAGE, stride=n_groups), :], ok_n)


def _group_rows(x, n_groups):
    b, t, dq = x.shape
    rep = dq // HD // n_groups
    return x.reshape(b, t, n_groups, rep, HD).transpose(0, 2, 3, 1, 4).reshape(b, n_groups, rep * t, HD)


def _ungroup_rows(o, t):
    b, g, rows, _ = o.shape
    rep = rows // t
    return o.reshape(b, g, rep, t, HD).transpose(0, 3, 1, 2, 4).reshape(b, t, g * rep * HD)


def _pad_new_rows(x, n_groups):
    b, t, _ = x.shape
    x = x.reshape(b, t * n_groups, HD)
    return jnp.pad(x, ((0, 0), (0, (PAGE - t) * n_groups), (0, 0)))


def dsa_sample(q, qi, wi, k_new, v_new, ik_new, cache_k, cache_v, cache_ik, pt, *, tk=1024):
    nb, nq, dq = q.shape
    n_pages = pt.shape[1]
    n_groups = cache_k.shape[2]
    rep = dq // HD // n_groups
    rows = rep * nq
    past = n_pages * PAGE
    topk = min(TOPK_MAX, (past + nq) // 4)
    width = past + PAGE
    rep_s = max(1, 8 // nq)
    rows_s = rep_s * nq
    assert rows % rows_s == 0
    qi_r = jnp.broadcast_to(qi.reshape(nb, nq, H_IDX, D_IDX).transpose(0, 2, 1, 3)[:, :, None],
                            (nb, H_IDX, rep_s, nq, D_IDX)).reshape(nb, H_IDX * rows_s, D_IDX)
    wi_r = jnp.broadcast_to(wi.transpose(0, 2, 1)[:, :, None],
                            (nb, H_IDX, rep_s, nq)).reshape(nb, H_IDX * rows_s, 1)
    ikn = jnp.swapaxes(jnp.pad(ik_new, ((0, 0), (0, PAGE - nq), (0, 0))), 1, 2)
    cache_ik = jnp.swapaxes(cache_ik, 1, 2)
    sel_kern = functools.partial(_dsa_sample_select_kernel, n_pages=n_pages, tk=tk, topk=topk,
                                 n_new=nq, nq=nq, rows=rows_s)
    mask = _paged_call(
        sel_kern, pt, [qi_r, wi_r, ikn, cache_ik], 1, F32, (rows_s, width),
        [pltpu.VMEM((2, D_IDX, past), F32), pltpu.VMEM((rows_s, width), I32),
         pltpu.SemaphoreType.DMA((2,))], "dsa_sample_select")
    ck = cache_k.reshape(cache_k.shape[0], PAGE * n_groups, HD)
    cv = cache_v.reshape(cache_v.shape[0], PAGE * n_groups, HD)
    att_kern = functools.partial(_dsa_sample_attn_kernel, n_pages=n_pages, n_groups=n_groups)
    o = _paged_call(
        att_kern, pt, [_group_rows(q, n_groups), mask, _pad_new_rows(k_new, n_groups),
                       _pad_new_rows(v_new, n_groups), ck, cv], 2, F32, (n_groups, rows, HD),
        [pltpu.VMEM((2, past * n_groups, HD), F32), pltpu.VMEM((2, past * n_groups, HD), F32),
         pltpu.SemaphoreType.DMA((2,)), pltpu.SemaphoreType.DMA((2,))], "dsa_sample_attn")
    return _ungroup_rows(o, nq)


def _nsa_compress_paged_kernel(pt_ref, pe_ref, w1_ref, w2_ref, x_hbm, o_ref, buf, sem,
                               *, n_pages, n_groups):
    slot = _fetch_pages([x_hbm], [buf], [sem], pt_ref, n_pages)
    _compress_rows(buf.at[slot], pe_ref, w1_ref, w2_ref, o_ref,
                   nblk=n_pages * PAGE // CMP_BLK, n_groups=n_groups)


def nsa_compress_paged(cache, pt, pe, w1, w2):
    n_groups = cache.shape[2]
    n_pages = pt.shape[1]
    past = n_pages * PAGE
    kern = functools.partial(_nsa_compress_paged_kernel, n_pages=n_pages, n_groups=n_groups)
    view = cache.reshape(cache.shape[0], PAGE * n_groups, HD)
    return _paged_call(
        kern, pt, [pe, w1.astype(BF16), w2, view], 1, F32, (past // CMP_BLK, n_groups * HD),
        [pltpu.VMEM((2, past * n_groups, HD), F32), pltpu.SemaphoreType.DMA((2,))],
        "nsa_compress_paged")


def _nsa_sample_kernel(pt_ref, q_ref, gl_ref, ck_ref, cv_ref, ex_ref, wk_ref, wv_ref, skn_ref, svn_ref,
                       wkn_ref, wvn_ref, ks_hbm, vs_hbm, o_ref, kbuf, vbuf, ksem, vsem,
                       *, n_pages, n_groups, nq, n_new):
    slot = _fetch_pages([ks_hbm, vs_hbm], [kbuf, vbuf], [ksem, vsem], pt_ref, n_pages)
    past = n_pages * PAGE
    scale = HD ** -0.5
    rows = q_ref.shape[1]
    rep = rows // nq
    nbc = ck_ref.shape[0]
    wlen = wk_ref.shape[0] // n_groups
    t_row = lax.rem(lax.broadcasted_iota(I32, (rows, 1), 0), nq)
    qpos = past + t_row
    jb = lax.broadcasted_iota(I32, (1, nbc), 1)
    cmp_ok = ((jb + 1) * CMP_BLK - 1) <= qpos
    cur = qpos // CMP_BLK
    newcol = lax.broadcasted_iota(I32, (1, PAGE), 1)
    new_ok = (newcol < n_new) & (newcol <= t_row)
    diff = qpos - (past - wlen + lax.broadcasted_iota(I32, (1, wlen), 1))
    win_ok = (diff >= 0) & (diff <= WINDOW)
    o_cs, scores = [], []
    for g in range(n_groups):
        cols = slice(g * HD, (g + 1) * HD)
        p_c = _masked_softmax(_dot_nt(q_ref[g].astype(BF16), ck_ref[:, cols]) * scale, cmp_ok)
        o_cs.append(_dot(p_c, cv_ref[:, cols]))
        imp = p_c
        for r in range(1, rep):
            imp = imp + pltpu.roll(p_c, r * nq, 0)
        scores.append(jnp.where(jb > cur, NEG, jnp.where((jb == cur) | (jb == 0), BIG, imp)))
    selm = _top_blocks(jnp.concatenate(scores, axis=0).T, N_SEL - 1, axis=0).T.astype(BF16)
    picked_all = jnp.dot(selm, ex_ref[...], preferred_element_type=F32)
    for g in range(n_groups):
        qg = q_ref[g].astype(BF16)
        gates = jax.nn.sigmoid(gl_ref[g])
        o_c = o_cs[g]
        picked = picked_all[g * rows:(g + 1) * rows] > 0.5
        o_s = _attend_two_parts(
            qg, kbuf[slot, pl.ds(g, past, stride=n_groups), :],
            vbuf[slot, pl.ds(g, past, stride=n_groups), :], picked,
            skn_ref[pl.ds(g, PAGE, stride=n_groups), :],
            svn_ref[pl.ds(g, PAGE, stride=n_groups), :], new_ok)
        o_w = _attend_two_parts(
            qg, wk_ref[pl.ds(g, wlen, stride=n_groups), :],
            wv_ref[pl.ds(g, wlen, stride=n_groups), :], win_ok,
            wkn_ref[pl.ds(g, PAGE, stride=n_groups), :],
            wvn_ref[pl.ds(g, PAGE, stride=n_groups), :], new_ok)
        o_ref[g] = gates[:, 0:1] * o_c + gates[:, 1:2] * o_s + gates[:, 2:3] * o_w


def nsa_sample(q, gl, ck, cv, win_k, win_v, ks_new, vs_new, kw_new, vw_new, cache_ks, cache_vs, pt):
    nb, nq, dq = q.shape
    n_pages = pt.shape[1]
    n_groups = cache_ks.shape[2]
    n_heads = dq // HD
    rep = n_heads // n_groups
    rows = rep * nq
    past = n_pages * PAGE
    assert past % CMP_BLK == 0 and nq <= CMP_BLK and ck.shape[1] == past // CMP_BLK
    assert win_k.shape[1] == WINDOW
    glr = gl.reshape(nb, nq, n_groups, rep, 3).transpose(0, 2, 3, 1, 4).reshape(nb, n_groups, rows, 3)
    view = lambda c: c.reshape(c.shape[0], c.shape[1] * n_groups, HD)
    kern = functools.partial(_nsa_sample_kernel, n_pages=n_pages, n_groups=n_groups,
                             nq=nq, n_new=nq)
    pad = lambda x: _pad_new_rows(x, n_groups)
    expand = (jnp.arange(past, dtype=I32)[None, :] // CMP_BLK
              == jnp.arange(past // CMP_BLK, dtype=I32)[:, None]).astype(BF16)
    o = _paged_call(
        kern, pt, [_group_rows(q, n_groups), glr, ck, cv, expand, view(win_k), view(win_v),
                   pad(ks_new), pad(vs_new), pad(kw_new), pad(vw_new), view(cache_ks), view(cache_vs)],
        2, F32, (n_groups, rows, HD),
        [pltpu.VMEM((2, past * n_groups, HD), F32), pltpu.VMEM((2, past * n_groups, HD), F32),
         pltpu.SemaphoreType.DMA((2,)), pltpu.SemaphoreType.DMA((2,))], "nsa_sample")
    return _ungroup_rows(o, nq)


def _tmajor(a):
    b, t, c = a.shape
    return jnp.swapaxes(a, 0, 1).reshape(t * b, c)


def _bmajor(a, b):
    tb, c = a.shape
    return jnp.swapaxes(a.reshape(tb // b, b, c), 0, 1)


def _col_splits(z, sizes):
    out, off = [], 0
    for s in sizes:
        out.append(z[..., off:off + s])
        off += s
    return out


def _row_tile(m, pref):
    return pref if m % pref == 0 else m


def kernel(x_prompt, x_sample, state_conv_a, cache_dsa_k, cache_dsa_v, cache_dsa_ik, cache_nsa_cmp_k, cache_nsa_cmp_v, cache_nsa_sel_k, cache_nsa_sel_v, cache_nsa_win_k, cache_nsa_win_v, state_conv_d, state_ffn_conv, cache_mem_k, cache_mem_v, page_table, mem_prompt, w_in_e, conv_a_w, w_out_e, w_in_o, nsa_pe_k, nsa_w1_k, nsa_w2_k, nsa_pe_v, nsa_w1_v, nsa_w2_v, conv_d_w, conv_d_b, ln_d_g, ln_d_b, w_out_o, norm_g, mem_norm_g, w_mq, w_mk, w_mv, w_mo, w_up, ffn_conv_w, w_down):
    bp, seq, d = x_prompt.shape
    nb, nq, _ = x_sample.shape
    assert bp == 1
    depth = norm_g.shape[0]
    ts = nb * nq
    d_a = conv_a_w.shape[-1]
    d_d = conv_d_w.shape[-1]
    d_ff = ffn_conv_w.shape[-1]
    kv_b = cache_dsa_k.shape[-2] * HD
    kv_c = cache_nsa_cmp_k.shape[-2] * HD
    n_kv_c = cache_nsa_cmp_k.shape[-2]
    dq_b = w_out_e.shape[1] - d_a
    dq_c = w_out_o.shape[1] - d_d
    n_gate = 3 * dq_c // HD
    split_e = (d_a, d_a, d_a, dq_b, kv_b, kv_b, H_IDX * D_IDX, D_IDX, H_IDX)
    split_o = (dq_c,) + (kv_c,) * 6 + (d_d, d_d, n_gate)
    gl_src = dq_c + 6 * kv_c
    w_in_o = jnp.concatenate([w_in_o[:, :, :gl_src], w_in_o[:, :, gl_src + n_gate:],
                              w_in_o[:, :, gl_src:gl_src + n_gate]], axis=2).astype(BF16)
    w_in_e, w_out_e, w_out_o, w_mq, w_mo, w_down = (
        w.astype(BF16) for w in (w_in_e, w_out_e, w_out_o, w_mq, w_mo, w_down))
    off_e = [sum(split_e[:j]) for j in range(len(split_e))]
    off_o = [sum(split_o[:j]) for j in range(len(split_o))]
    mem_g = mem_norm_g.reshape(depth, 1, d)
    tm_big = _row_tile(seq, 1024)
    tm_mid = _row_tile(seq, 512)
    wb = min(WINDOW, seq)

    yp = x_prompt.reshape(seq, d)
    ys = _tmajor(x_sample)
    mem = mem_prompt.reshape(mem_prompt.shape[1], d)
    names = ('p_conv_a', 'p_dsa_k', 'p_dsa_v', 'p_dsa_ik', 'p_cmp_k', 'p_cmp_v', 'p_sel_k', 'p_sel_v',
             'p_win_k', 'p_win_v', 'p_conv_d', 'p_ffn', 'p_mem_k', 'p_mem_v',
             's_conv_a', 's_dsa_k', 's_dsa_v', 's_dsa_ik', 's_cmp_k', 's_cmp_v', 's_sel_k', 's_sel_v',
             's_win_k', 's_win_v', 's_conv_d', 's_ffn')
    st = {n: [] for n in names}

    for l in range(depth):
        g = norm_g
        i = l // 2
        if l % 2 == 0:
            zp = norm_mm(yp, g, l, 0, w_in_e, i, tm=tm_big, tn=1024)
            zs = norm_mm(ys, g, l, 0, w_in_e, i, tm=ts, tn=512)
            kw_a = conv_a_w.shape[1]
            o_a, hist = gated_conv(zp, d_a, conv_a_w[i], jnp.zeros((_halo_rows(kw_a, 1), d_a), F32),
                                   tm=tm_mid, shift=1)
            st['p_conv_a'].append(hist[hist.shape[0] - (kw_a - 1):][None])
            k, v, ik = lax.optimization_barrier(
                tuple(zp[:, off_e[j]:off_e[j] + split_e[j]] for j in (4, 5, 7)))
            o_b = dsa_prompt(zp, off_e[3], off_e[6], off_e[7], dq_b // HD,
                             k.astype(BF16), v.astype(BF16), ik.astype(BF16).T)
            yp = mm_norm_res([o_a, o_b], w_out_e, i, g, l, 1, yp, tm=tm_mid, tn=512)
            st['p_dsa_k'].append(k.reshape(1, seq, -1, HD))
            st['p_dsa_v'].append(v.reshape(1, seq, -1, HD))
            st['p_dsa_ik'].append(ik[None])
            o_a, hist = gated_conv(zs, d_a, conv_a_w[i], _tmajor(state_conv_a[i]), tm=ts, shift=nb)
            st['s_conv_a'].append(_bmajor(hist, nb))
            _, _, _, q, k, v, qi, ik, wi = _col_splits(_bmajor(zs, nb), split_e)
            o_b = dsa_sample(q, qi, wi, k, v, ik, cache_dsa_k[i], cache_dsa_v[i], cache_dsa_ik[i],
                             page_table)
            ys = mm_norm_res([o_a, _tmajor(o_b)], w_out_e, i, g, l, 1, ys, tm=ts, tn=512)
            st['s_dsa_k'].append(k.reshape(nb, nq, -1, HD))
            st['s_dsa_v'].append(v.reshape(nb, nq, -1, HD))
            st['s_dsa_ik'].append(ik)
        else:
            zp = norm_mm(yp, g, l, 0, w_in_o, i, tm=tm_big, tn=1024)
            zs = norm_mm(ys, g, l, 0, w_in_o, i, tm=ts, tn=512)
            phi_k = (nsa_pe_k[i], nsa_w1_k[i], nsa_w2_k[i])
            phi_v = (nsa_pe_v[i], nsa_w1_v[i], nsa_w2_v[i])
            kw_d = conv_d_w.shape[1]
            kc, vc, ks, vs, kw, vw = lax.optimization_barrier(
                tuple(zp[:, off_o[j]:off_o[j] + kv_c] for j in range(1, 7)))
            rows2 = lambda a: a.reshape(1, seq * n_kv_c, HD)
            by_group = lambda c: c.reshape(-1, n_kv_c, HD).transpose(1, 0, 2).astype(BF16)
            ck = by_group(nsa_compress(rows2(kc), *phi_k, n_kv_c))
            cv = by_group(nsa_compress(rows2(vc), *phi_v, n_kv_c))
            o_c = nsa_prompt(zp, off_o[0], off_o[9], dq_c // HD, ck, cv, ks.astype(BF16),
                             vs.astype(BF16), kw.astype(BF16), vw.astype(BF16))
            o_d, hist = conformer_conv(zp, off_o[7], off_o[8], conv_d_w[i], conv_d_b[i], ln_d_g[i],
                                       ln_d_b[i], jnp.zeros((_halo_rows(kw_d, 1), d_d), F32),
                                       tm=tm_mid, shift=1)
            st['p_conv_d'].append(hist[hist.shape[0] - (kw_d - 1):][None])
            yp = mm_norm_res([o_c, o_d], w_out_o, i, g, l, 1, yp, tm=tm_mid, tn=512)
            kv4 = lambda a: a.reshape(1, -1, n_kv_c, HD)
            for n, a in zip(('p_cmp_k', 'p_cmp_v', 'p_sel_k', 'p_sel_v'), (kc, vc, ks, vs)):
                st[n].append(kv4(a))
            st['p_win_k'].append(kv4(kw[seq - wb:]))
            st['p_win_v'].append(kv4(vw[seq - wb:]))
            q, kc, vc, ks, vs, kw, vw, _, _, gl = _col_splits(_bmajor(zs, nb), split_o)
            ck = nsa_compress_paged(cache_nsa_cmp_k[i], page_table, *phi_k)
            cv = nsa_compress_paged(cache_nsa_cmp_v[i], page_table, *phi_v)
            win_k, win_v = cache_nsa_win_k[i], cache_nsa_win_v[i]
            o_c = nsa_sample(q, gl, ck, cv, win_k, win_v, ks, vs, kw, vw,
                             cache_nsa_sel_k[i], cache_nsa_sel_v[i], page_table)
            o_d, hist = conformer_conv(zs, off_o[7], off_o[8], conv_d_w[i], conv_d_b[i], ln_d_g[i],
                                       ln_d_b[i], _tmajor(state_conv_d[i]), tm=ts, shift=nb)
            st['s_conv_d'].append(_bmajor(hist, nb))
            ys = mm_norm_res([_tmajor(o_c), o_d], w_out_o, i, g, l, 1, ys, tm=ts, tn=512)
            kv4 = lambda a: a.reshape(nb, nq, n_kv_c, HD)
            for n, a in zip(('s_cmp_k', 's_cmp_v', 's_sel_k', 's_sel_v'), (kc, vc, ks, vs)):
                st[n].append(kv4(a))
            st['s_win_k'].append(jnp.concatenate([win_k, kv4(kw)], axis=1)[:, nq:])
            st['s_win_v'].append(jnp.concatenate([win_v, kv4(vw)], axis=1)[:, nq:])

        n_mem = mem.shape[0]
        mk = norm_mm(mem, mem_g, l, 0, w_mk, l, tm=n_mem, tn=512)
        mv = norm_mm(mem, mem_g, l, 0, w_mv, l, tm=n_mem, tn=512)
        st['p_mem_k'].append(mk.reshape(1, n_mem, -1, HD))
        st['p_mem_v'].append(mv.reshape(1, n_mem, -1, HD))
        yp = cross_attn(yp, g, l, 2, 3, w_mq, w_mo, l, mk, mv, tm=tm_mid)
        qs = norm_mm(ys, g, l, 2, w_mq, l, tm=ts, tn=512)
        a = mem_attn_batched(_bmajor(qs, nb), cache_mem_k, cache_mem_v, l)
        ys = mm_norm_res([_tmajor(a)], w_mo, l, g, l, 3, ys, tm=ts, tn=512)

        kw_f = ffn_conv_w.shape[1]
        gact, hist = ffn_up(yp, g, l, 4, w_up, l, ffn_conv_w[l],
                            jnp.zeros((_halo_rows(kw_f, 1), d_ff), F32), tm=tm_big, shift=1)
        st['p_ffn'].append(hist[hist.shape[0] - (kw_f - 1):, :d_ff][None])
        yp = mm_norm_res([gact], w_down, l, g, l, 5, yp, tm=tm_mid, tn=512, widths=(d_ff,))
        gact, hist = ffn_up(ys, g, l, 4, w_up, l, ffn_conv_w[l], _tmajor(state_ffn_conv[l]),
                            tm=ts, shift=nb)
        st['s_ffn'].append(_bmajor(hist[:, :d_ff], nb))
        ys = mm_norm_res([gact], w_down, l, g, l, 5, ys, tm=ts, tn=512, widths=(d_ff,))

    out = {n: jnp.stack(a) for n, a in st.items()}
    return (yp.reshape(1, seq, d), _bmajor(ys, nb)) + tuple(out[n] for n in names)
```
